```python
import math
import jax, jax.numpy as jnp
from jax import lax
import numpy as np

D_MODEL = 1024
BATCH = 8
SEQ = 2048
DEPTH = 1

M_HEADS = 4
M_HEAD_DIM = D_MODEL // M_HEADS
D_M = M_HEADS * M_HEAD_DIM
QK_CONV = 4
CHUNK = 64
D_C = D_MODEL
CONF_CONV = 31
D_FF = ((8 * D_MODEL // 3 + 255) // 256) * 256
PLE_DIM = 256
EPS = 1e-6
SPLITS = (D_M, 2 * D_M, 3 * D_M, 4 * D_M, 4 * D_M + M_HEADS, 4 * D_M + 2 * M_HEADS, 4 * D_M + 2 * M_HEADS + 2 * D_C)
N_IN = 4 * D_M + 2 * M_HEADS + 2 * D_C + 2 * D_MODEL

kernel_name = 'hybrid_mlstm_conformer_conv_gated_block'


def rmsnorm(x, g):
    xf = x.astype(jnp.float32)
    y = xf * lax.rsqrt(jnp.mean(xf * xf, axis=-1, keepdims=True) + EPS) * g.astype(jnp.float32)
    return y.astype(x.dtype)


def layernorm(x, g, b):
    xf = x.astype(jnp.float32)
    mu = jnp.mean(xf, axis=-1, keepdims=True)
    var = jnp.mean(jnp.square(xf - mu), axis=-1, keepdims=True)
    y = (xf - mu) * lax.rsqrt(var + EPS) * g.astype(jnp.float32) + b.astype(jnp.float32)
    return y.astype(x.dtype)


def causal_dwconv(x, w, b):
    K, C = w.shape
    y = lax.conv_general_dilated(x, w[:, None, :].astype(x.dtype), window_strides=(1,), padding=[(K - 1, 0)], dimension_numbers=('NWC', 'WIO', 'NWC'), feature_group_count=C)
    return y + b.astype(x.dtype)


def mlstm_chunkwise(q, k, v, ig, lf):
    B, S, H, dk = q.shape
    dv = v.shape[-1]
    nc = S // CHUNK
    f32 = jnp.float32

    def to_chunks(t):
        return t.astype(f32).reshape(B, nc, CHUNK, H, -1).transpose(1, 0, 3, 2, 4)

    def gate_chunks(t):
        return t.astype(f32).reshape(B, nc, CHUNK, H).transpose(1, 0, 3, 2)

    causal = jnp.tril(jnp.ones((CHUNK, CHUNK), dtype=bool))

    def step(carry, inp):
        C, n, m = carry
        qc, kc, vc, igc, lfc = inp
        b = jnp.cumsum(lfc, axis=-1)
        D = b[..., :, None] - b[..., None, :] + igc[..., None, :]
        D = jnp.where(causal, D, -jnp.inf)
        inter = b + m[..., None]
        m_t = jnp.maximum(jnp.max(D, axis=-1), inter)
        s = jnp.einsum('bhtd,bhsd->bhts', qc, kc) * jnp.exp(D - m_t[..., None])
        w_inter = jnp.exp(inter - m_t)
        num = jnp.einsum('bhts,bhse->bhte', s, vc) + w_inter[..., None] * jnp.einsum('bhtd,bhde->bhte', qc, C)
        den = jnp.sum(s, axis=-1) + w_inter * jnp.einsum('bhtd,bhd->bht', qc, n)
        h = num / jnp.maximum(jnp.abs(den), jnp.exp(-m_t))[..., None]
        bL = b[..., -1]
        g = bL[..., None] - b + igc
        m_new = jnp.maximum(bL + m, jnp.max(g, axis=-1))
        wk = jnp.exp(g - m_new[..., None])
        decay = jnp.exp(bL + m - m_new)
        C_new = decay[..., None, None] * C + jnp.einsum('bhs,bhsd,bhse->bhde', wk, kc, vc)
        n_new = decay[..., None] * n + jnp.einsum('bhs,bhsd->bhd', wk, kc)
        return (C_new, n_new, m_new), h

    init = (jnp.zeros((B, H, dk, dv), f32), jnp.zeros((B, H, dk), f32), jnp.zeros((B, H), f32))
    _, h = lax.scan(step, init, (to_chunks(q), to_chunks(k), to_chunks(v), gate_chunks(ig), gate_chunks(lf)))
    return h.transpose(1, 0, 3, 2, 4).reshape(B, S, H * dv)


def setup_inputs(seed: int = 0) -> dict:
    key = jax.random.key(seed)
    ks = jax.random.split(key, 32)
    f32 = jnp.float32

    def nrm(k, shape, scale):
        return jax.random.normal(k, shape, f32) * scale

    def gain(k, shape):
        return 1.0 + nrm(k, shape, 0.02)

    b_if = jnp.concatenate([nrm(ks[3], (DEPTH, M_HEADS), 0.1), jnp.linspace(3.0, 6.0, M_HEADS, dtype=f32)[None, :] + nrm(ks[4], (DEPTH, M_HEADS), 0.1)], axis=-1)
    return {
        'x': nrm(ks[0], (BATCH, SEQ, D_MODEL), 1.0),
        'p': nrm(ks[1], (DEPTH, BATCH, SEQ, PLE_DIM), 1.0),
        'norm_mix_g': gain(ks[2], (DEPTH, D_MODEL)),
        'w_in': nrm(ks[5], (DEPTH, D_MODEL, N_IN), D_MODEL ** -0.5),
        'b_if': b_if,
        'conv_qk_w': nrm(ks[6], (DEPTH, QK_CONV, 2 * D_M), QK_CONV ** -0.5),
        'conv_qk_b': nrm(ks[7], (DEPTH, 2 * D_M), 0.02),
        'mh_norm_g': gain(ks[8], (DEPTH, D_M)),
        'conf_conv_w': nrm(ks[9], (DEPTH, CONF_CONV, D_C), CONF_CONV ** -0.5),
        'conf_conv_b': nrm(ks[10], (DEPTH, D_C), 0.02),
        'conf_ln_g': gain(ks[11], (DEPTH, D_C)),
        'conf_ln_b': nrm(ks[12], (DEPTH, D_C), 0.02),
        'w_branch_m': nrm(ks[13], (DEPTH, D_M, D_MODEL), D_M ** -0.5),
        'w_branch_c': nrm(ks[14], (DEPTH, D_C, D_MODEL), D_C ** -0.5),
        'w_out': nrm(ks[15], (DEPTH, D_MODEL, D_MODEL), D_MODEL ** -0.5),
        'norm_ffn_g': gain(ks[16], (DEPTH, D_MODEL)),
        'w_ffn_gate': nrm(ks[17], (DEPTH, D_MODEL, D_FF), D_MODEL ** -0.5),
        'w_ffn_up': nrm(ks[18], (DEPTH, D_MODEL, D_FF), D_MODEL ** -0.5),
        'w_ffn_down': nrm(ks[19], (DEPTH, D_FF, D_MODEL), D_FF ** -0.5),
        'norm_ple_g': gain(ks[20], (DEPTH, D_MODEL)),
        'w_ple_gate': nrm(ks[21], (DEPTH, D_MODEL, D_MODEL), D_MODEL ** -0.5),
        'w_ple_proj': nrm(ks[22], (DEPTH, PLE_DIM, D_MODEL), PLE_DIM ** -0.5),
        'final_g': gain(ks[23], (D_MODEL,)),
    }


def reference(x, p, norm_mix_g, w_in, b_if, conv_qk_w, conv_qk_b, mh_norm_g, conf_conv_w, conf_conv_b, conf_ln_g, conf_ln_b, w_branch_m, w_branch_c, w_out, norm_ffn_g, w_ffn_gate, w_ffn_up, w_ffn_down, norm_ple_g, w_ple_gate, w_ple_proj, final_g):
    B, S, _ = x.shape
    for l in range(DEPTH):
        h = rmsnorm(x, norm_mix_g[l])
        proj = h @ w_in[l]
        q_pre, k_pre, v, o_pre, i_pre, f_pre, conf_in, gate_pre = jnp.split(proj, SPLITS, axis=-1)

        qk = jax.nn.silu(causal_dwconv(jnp.concatenate([q_pre, k_pre], axis=-1), conv_qk_w[l], conv_qk_b[l]))
        q, k = jnp.split(qk, 2, axis=-1)
        q = q.reshape(B, S, M_HEADS, M_HEAD_DIM) * (M_HEAD_DIM ** -0.5)
        k = k.reshape(B, S, M_HEADS, M_HEAD_DIM)
        v = v.reshape(B, S, M_HEADS, M_HEAD_DIM)
        i_b, f_b = jnp.split(b_if[l], 2)
        ig = i_pre.astype(jnp.float32) + i_b.astype(jnp.float32)
        lf = jax.nn.log_sigmoid(f_pre.astype(jnp.float32) + f_b.astype(jnp.float32))
        hm = mlstm_chunkwise(q, k, v, ig, lf).astype(x.dtype)
        hm = rmsnorm(hm.reshape(B, S, M_HEADS, M_HEAD_DIM), mh_norm_g[l].reshape(M_HEADS, M_HEAD_DIM)).reshape(B, S, D_M)
        hm = hm * jax.nn.sigmoid(o_pre)
        branch_m = hm @ w_branch_m[l]

        a, ga = jnp.split(conf_in, 2, axis=-1)
        u = a * jax.nn.sigmoid(ga)
        u = causal_dwconv(u, conf_conv_w[l], conf_conv_b[l])
        u = jax.nn.silu(layernorm(u, conf_ln_g[l], conf_ln_b[l]))
        branch_c = u @ w_branch_c[l]

        g_m, g_c = jnp.split(jax.nn.sigmoid(gate_pre), 2, axis=-1)
        x = x + (g_m * branch_m + g_c * branch_c) @ w_out[l]

        f = rmsnorm(x, norm_ffn_g[l])
        x = x + (jax.nn.silu(f @ w_ffn_gate[l]) * (f @ w_ffn_up[l])) @ w_ffn_down[l]

        gate = jax.nn.sigmoid(rmsnorm(x, norm_ple_g[l]) @ w_ple_gate[l])
        x = x + gate * (p[l].astype(x.dtype) @ w_ple_proj[l])
    return rmsnorm(x, final_g)
```

```python
import functools

import jax
import jax.numpy as jnp
from jax import lax
from jax.experimental import pallas as pl
from jax.experimental.pallas import tpu as pltpu

EPS = 1e-6
M_HEADS = 4
MLSTM_CHUNK = 128
SUBLANES = 8
LANES = 128
CONF_HALO = 32
VMEM_LIMIT = 56 * 1024 * 1024

BF16 = jnp.bfloat16
F32 = jnp.float32


def _sigmoid(z):
    return 0.5 * jnp.tanh(0.5 * z) + 0.5


def _rms(xf, g):
    return xf * lax.rsqrt(jnp.mean(xf * xf, axis=-1, keepdims=True) + EPS) * g


def _resident(shape):
    nd = len(shape)
    return pl.BlockSpec(shape, lambda *_: (0,) * nd, pipeline_mode=pl.Buffered(1))


def _inproj_kernel(x_ref, g_ref, w_ref, wif_ref, cw_ref, cb_ref,
                   q_ref, k_ref, v_ref, so_ref, u_ref, gate_ref, if_ref,
                   cbuf, *, tm, d, tiles_per_seq, q_scale):
    i = pl.program_id(0)
    taps = cw_ref.shape[0]

    @pl.when(i == 0)
    def _():
        cbuf[pl.ds(tm, SUBLANES), :] = jnp.zeros((SUBLANES, 2 * d), F32)

    h = _rms(x_ref[...], g_ref[...]).astype(BF16)

    halo = jnp.where(i % tiles_per_seq == 0, 0.0, cbuf[pl.ds(tm, SUBLANES), :])
    cbuf[pl.ds(0, SUBLANES), :] = halo
    cbuf[pl.ds(SUBLANES, tm), :] = jnp.dot(h, w_ref[:, 0:2 * d], preferred_element_type=F32)
    y = cb_ref[...]
    for t in range(taps):
        y = y + cw_ref[t:t + 1, :] * cbuf[pl.ds(SUBLANES - (taps - 1) + t, tm), :]
    y = y * _sigmoid(y)
    q_ref[...] = (y[:, :d] * q_scale).astype(BF16)
    k_ref[...] = y[:, d:].astype(BF16)

    pvo = jnp.dot(h, w_ref[:, 2 * d:4 * d], preferred_element_type=F32)
    v_ref[...] = pvo[:, :d].astype(BF16)
    so_ref[...] = _sigmoid(pvo[:, d:]).astype(BF16)

    pglu = jnp.dot(h, w_ref[:, 4 * d:6 * d], preferred_element_type=F32)
    u_ref[...] = (pglu[:, :d] * _sigmoid(pglu[:, d:])).astype(BF16)

    pg = jnp.dot(h, w_ref[:, 6 * d:8 * d], preferred_element_type=F32)
    gate_ref[...] = _sigmoid(pg).astype(BF16)

    if_ref[...] = jnp.dot(h, wif_ref[...], preferred_element_type=F32)


def _inproj(x2, g, w_big, w_if, conv_w, conv_b, *, seq, tm):
    t, d = x2.shape
    kern = functools.partial(_inproj_kernel, tm=tm, d=d, tiles_per_seq=seq // tm,
                             q_scale=float((d // M_HEADS) ** -0.5))
    tok = lambda width: pl.BlockSpec((tm, width), lambda i: (i, 0))
    out_shape = [jax.ShapeDtypeStruct((t, d), BF16)] * 5 + [
        jax.ShapeDtypeStruct((t, 2 * d), BF16), jax.ShapeDtypeStruct((t, LANES), F32)]
    return pl.pallas_call(
        kern,
        grid=(t // tm,),
        in_specs=[tok(d), _resident(g.shape), _resident(w_big.shape), _resident(w_if.shape),
                  _resident(conv_w.shape), _resident(conv_b.shape)],
        out_specs=[tok(d)] * 5 + [tok(2 * d), tok(LANES)],
        out_shape=out_shape,
        scratch_shapes=[pltpu.VMEM((tm + SUBLANES, 2 * d), F32)],
        compiler_params=pltpu.CompilerParams(
            dimension_semantics=("arbitrary",), vmem_limit_bytes=VMEM_LIMIT),
        name="inproj",
    )(x2, g, w_big, w_if, conv_w, conv_b)


def _mlstm_kernel(q_ref, k_ref, v_ref, so_ref, gz_ref, bif_ref, ng_ref, o_ref,
                  c_ref, m_ref, *, ts, dh):
    L = MLSTM_CHUNK
    nh = M_HEADS
    j = pl.program_id(1)

    @pl.when(j == 0)
    def _():
        c_ref[...] = jnp.zeros(c_ref.shape, F32)
        m_ref[...] = jnp.zeros(m_ref.shape, F32)

    row = lax.broadcasted_iota(jnp.int32, (L, L), 0)
    col = lax.broadcasted_iota(jnp.int32, (L, L), 1)
    causal = col <= row
    eye = col == row
    lane8 = lax.broadcasted_iota(jnp.int32, (2 * nh, L), 1)
    ones_col = (lax.broadcasted_iota(jnp.int32, (L, LANES), 1) == 0).astype(BF16)

    def chunk(ci, carry):
        r0 = pl.multiple_of(ci * L, L)
        pre = gz_ref[0, :, pl.ds(r0, L)] + bif_ref[...]
        lf = jnp.minimum(pre, 0.0) - jnp.log(1.0 + jnp.exp(-jnp.abs(pre)))
        b = lf
        sh = 1
        while sh < L:
            b = b + jnp.where(lane8 >= sh, pltpu.roll(b, sh, axis=1), 0.0)
            sh *= 2
        for hh in range(nh):
            cs = slice(hh * dh, (hh + 1) * dh)
            b_row = b[nh + hh:nh + hh + 1, :]
            a_row = pre[hh:hh + 1, :] - b_row
            a_mat = jnp.broadcast_to(a_row, (L, L))
            m_prev = m_ref[hh:hh + 1, 0:1]
            c_col = jnp.maximum(
                jnp.max(jnp.where(causal, a_mat, -jnp.inf), axis=-1, keepdims=True), m_prev)
            b_col = jnp.sum(jnp.where(eye, jnp.broadcast_to(b_row, (L, L)), 0.0),
                            axis=-1, keepdims=True)
            p_mat = jnp.exp(jnp.where(causal, a_mat - c_col, -jnp.inf))
            w_inter = jnp.exp(m_prev - c_col)

            qh = q_ref[pl.ds(r0, L), cs]
            kh = k_ref[pl.ds(r0, L), cs]
            v_aug = jnp.concatenate([v_ref[pl.ds(r0, L), cs], ones_col], axis=-1)
            s = lax.dot_general(qh, kh, (((1,), (1,)), ((), ())), preferred_element_type=F32)
            sg = (s * p_mat).astype(BF16)
            c_aug = c_ref[hh]
            r = (jnp.dot(sg, v_aug, preferred_element_type=F32)
                 + w_inter * jnp.dot(qh, c_aug.astype(BF16), preferred_element_type=F32))
            den = r[:, dh:dh + 1]
            m_t = b_col + c_col
            hv = r[:, :dh] * (1.0 / jnp.maximum(jnp.abs(den), jnp.exp(-m_t)))
            hn = _rms(hv, ng_ref[:, cs])
            o_ref[pl.ds(r0, L), cs] = (hn * so_ref[pl.ds(r0, L), cs].astype(F32)).astype(BF16)

            c_last = c_col[L - 1:L, :]
            wk_row = jnp.exp(a_row - c_last)
            decay = jnp.exp(m_prev - c_last)
            kt = (kh.astype(F32).T * wk_row).astype(BF16)
            c_ref[hh] = decay * c_aug + jnp.dot(kt, v_aug, preferred_element_type=F32)
            m_new = b_row[:, L - 1:L] + c_last
            m_ref[hh:hh + 1, :] = jnp.broadcast_to(m_new, (1, LANES))
        return carry

    lax.fori_loop(0, ts // L, chunk, 0)


def _mlstm(q, k, v, so, gz, bif, ng, *, batch, seq, ts):
    t, dm = q.shape
    dh = dm // M_HEADS
    ns = seq // ts
    tok = pl.BlockSpec((ts, dm), lambda b, j: (b * ns + j, 0))
    kern = functools.partial(_mlstm_kernel, ts=ts, dh=dh)
    return pl.pallas_call(
        kern,
        grid=(batch, ns),
        in_specs=[tok, tok, tok, tok,
                  pl.BlockSpec((1, 2 * M_HEADS, ts), lambda b, j: (b, 0, j)),
                  _resident(bif.shape), _resident(ng.shape)],
        out_specs=tok,
        out_shape=jax.ShapeDtypeStruct((t, dm), BF16),
        scratch_shapes=[pltpu.VMEM((M_HEADS, dh, dh + LANES), F32),
                        pltpu.VMEM((SUBLANES, LANES), F32)],
        compiler_params=pltpu.CompilerParams(
            dimension_semantics=("arbitrary", "arbitrary"), vmem_limit_bytes=VMEM_LIMIT),
        name="mlstm",
    )(q, k, v, so, gz, bif, ng)


def _merge_kernel(x_ref, hm_ref, u_ref, gate_ref, cw_ref, cb_ref, lg_ref, lb_ref,
                  wm_ref, wc_ref, wo_ref, o_ref, ubuf, *, tm, d, tiles_per_seq):
    i = pl.program_id(0)
    taps = cw_ref.shape[0]

    @pl.when(i == 0)
    def _():
        ubuf[pl.ds(tm, CONF_HALO), :] = jnp.zeros((CONF_HALO, d), F32)

    halo = jnp.where(i % tiles_per_seq == 0, 0.0, ubuf[pl.ds(tm, CONF_HALO), :])
    ubuf[pl.ds(0, CONF_HALO), :] = halo
    ubuf[pl.ds(CONF_HALO, tm), :] = u_ref[...].astype(F32)
    y = cb_ref[...]
    for t in range(taps):
        y = y + cw_ref[t:t + 1, :] * ubuf[pl.ds(CONF_HALO - (taps - 1) + t, tm), :]
    mu = jnp.mean(y, axis=-1, keepdims=True)
    yc = y - mu
    var = jnp.mean(yc * yc, axis=-1, keepdims=True)
    z = yc * lax.rsqrt(var + EPS) * lg_ref[...] + lb_ref[...]
    uc = (z * _sigmoid(z)).astype(BF16)

    bm = jnp.dot(hm_ref[...], wm_ref[...], preferred_element_type=F32)
    bc = jnp.dot(uc, wc_ref[...], preferred_element_type=F32)
    mix = gate_ref[:, :d].astype(F32) * bm + gate_ref[:, d:].astype(F32) * bc
    o_ref[...] = x_ref[...] + jnp.dot(mix.astype(BF16), wo_ref[...], preferred_element_type=F32)


def _merge(x2, hm, u, gates, cw, cb, lg, lb, wm, wc, wo, *, seq, tm):
    t, d = x2.shape
    kern = functools.partial(_merge_kernel, tm=tm, d=d, tiles_per_seq=seq // tm)
    tok = lambda width: pl.BlockSpec((tm, width), lambda i: (i, 0))
    return pl.pallas_call(
        kern,
        grid=(t // tm,),
        in_specs=[tok(d), tok(d), tok(d), tok(2 * d)]
        + [_resident(a.shape) for a in (cw, cb, lg, lb, wm, wc, wo)],
        out_specs=tok(d),
        out_shape=jax.ShapeDtypeStruct((t, d), F32),
        scratch_shapes=[pltpu.VMEM((tm + CONF_HALO, d), F32)],
        compiler_params=pltpu.CompilerParams(
            dimension_semantics=("arbitrary",), vmem_limit_bytes=VMEM_LIMIT),
        name="merge",
    )(x2, hm, u, gates, cw, cb, lg, lb, wm, wc, wo)


def _tail_kernel(x_ref, p_ref, gf_ref, wg_ref, wu_ref, wd_ref, gp_ref, wpg_ref, wpp_ref,
                 fg_ref, o_ref, *, ff_splits, final_norm):
    x1 = x_ref[...]
    f = _rms(x1, gf_ref[...]).astype(BF16)
    dff = wg_ref.shape[1]
    step = dff // ff_splits
    acc = x1
    for s in range(ff_splits):
        sl = slice(s * step, (s + 1) * step)
        gt = jnp.dot(f, wg_ref[:, sl], preferred_element_type=F32)
        up = jnp.dot(f, wu_ref[:, sl], preferred_element_type=F32)
        hid = (gt * _sigmoid(gt) * up).astype(BF16)
        acc = acc + jnp.dot(hid, wd_ref[sl, :], preferred_element_type=F32)
    x2 = acc
    gate = _sigmoid(jnp.dot(_rms(x2, gp_ref[...]).astype(BF16), wpg_ref[...],
                            preferred_element_type=F32))
    x3 = x2 + gate * jnp.dot(p_ref[...].astype(BF16), wpp_ref[...], preferred_element_type=F32)
    o_ref[...] = _rms(x3, fg_ref[...]) if final_norm else x3


def _tail(x1, p2, gf, wg, wu, wd, gp, wpg, wpp, fg, *, tm, final_norm):
    t, d = x1.shape
    dff = wg.shape[1]
    ff_splits = 2 if (dff // 2) % LANES == 0 else 1
    kern = functools.partial(_tail_kernel, ff_splits=ff_splits, final_norm=final_norm)
    tok = lambda width: pl.BlockSpec((tm, width), lambda i: (i, 0))
    return pl.pallas_call(
        kern,
        grid=(t // tm,),
        in_specs=[tok(d), tok(p2.shape[1])]
        + [_resident(a.shape) for a in (gf, wg, wu, wd, gp, wpg, wpp, fg)],
        out_specs=tok(d),
        out_shape=jax.ShapeDtypeStruct((t, d), F32),
        compiler_params=pltpu.CompilerParams(
            dimension_semantics=("arbitrary",), vmem_limit_bytes=VMEM_LIMIT),
        name="tail",
    )(x1, p2, gf, wg, wu, wd, gp, wpg, wpp, fg)


def kernel(x, p, norm_mix_g, w_in, b_if, conv_qk_w, conv_qk_b, mh_norm_g, conf_conv_w, conf_conv_b, conf_ln_g, conf_ln_b, w_branch_m, w_branch_c, w_out, norm_ffn_g, w_ffn_gate, w_ffn_up, w_ffn_down, norm_ple_g, w_ple_gate, w_ple_proj, final_g):
    batch, seq, d = x.shape
    depth = w_in.shape[0]
    t = batch * seq
    nh = M_HEADS
    tm = 512
    assert seq % tm == 0 and seq % MLSTM_CHUNK == 0 and d % (nh * LANES) == 0
    row = lambda a: a.reshape(1, -1).astype(F32)

    xf = x.reshape(t, d)
    for l in range(depth):
        w = w_in[l]
        n_gate = 2 * nh
        w_big = jnp.concatenate([w[:, :4 * d], w[:, 4 * d + n_gate:]], axis=1).astype(BF16)
        w_if = jnp.pad(w[:, 4 * d:4 * d + n_gate], ((0, 0), (0, LANES - n_gate))).astype(BF16)

        q, k, v, so, u, gates, ifp = _inproj(
            xf, row(norm_mix_g[l]), w_big, w_if, conv_qk_w[l].astype(F32), row(conv_qk_b[l]),
            seq=seq, tm=tm)

        gz = ifp[:, :n_gate].reshape(batch, seq, n_gate).transpose(0, 2, 1)
        hm = _mlstm(q, k, v, so, gz, b_if[l].reshape(n_gate, 1).astype(F32), row(mh_norm_g[l]),
                    batch=batch, seq=seq, ts=tm)

        x1 = _merge(xf, hm, u, gates, conf_conv_w[l].astype(F32), row(conf_conv_b[l]),
                    row(conf_ln_g[l]), row(conf_ln_b[l]), w_branch_m[l].astype(BF16),
                    w_branch_c[l].astype(BF16), w_out[l].astype(BF16), seq=seq, tm=tm)

        xf = _tail(x1, p[l].reshape(t, -1), row(norm_ffn_g[l]), w_ffn_gate[l].astype(BF16),
                   w_ffn_up[l].astype(BF16), w_ffn_down[l].astype(BF16), row(norm_ple_g[l]),
                   w_ple_gate[l].astype(BF16), w_ple_proj[l].astype(BF16), row(final_g),
                   tm=tm, final_norm=(l == depth - 1))
    return xf.reshape(batch, seq, d)
```

```python
import functools

import jax
import jax.numpy as jnp
from jax import lax
from jax.experimental import pallas as pl
from jax.experimental.pallas import tpu as pltpu

EPS = 1e-6
M_HEADS = 4
MLSTM_CHUNK = 128
SUBLANES = 8
LANES = 128
GATHER_STRIDE = 4
GATHER_ROWS = 16
QK_HALO = 8
CONF_HALO = 32
VMEM_LIMIT = 60 * 1024 * 1024

BF16 = jnp.bfloat16
F32 = jnp.float32


def _sigmoid(z):
    return 0.5 * jnp.tanh(0.5 * z) + 0.5


def _silu(z):
    h = 0.5 * z
    return h * jnp.tanh(h) + h


def _rms(xf, g):
    return xf * lax.rsqrt(jnp.mean(xf * xf, axis=-1, keepdims=True) + EPS) * g


def _resident(shape):
    nd = len(shape)
    return pl.BlockSpec(shape, lambda *_: (0,) * nd, pipeline_mode=pl.Buffered(1))


def _fill_slabs(buf, val, first, *, halo, rows):
    for s in range(val.shape[1] // LANES):
        prev = buf[s, pl.ds(rows, halo), :]
        buf[s, pl.ds(0, halo), :] = jnp.where(first, 0.0, prev)
        buf[s, pl.ds(halo, rows), :] = val[:, s * LANES:(s + 1) * LANES]


def _causal_conv(src, dst, w_ref, b_ref, *, nslab, rows, halo):
    taps = w_ref.shape[0]
    st = GATHER_STRIDE
    span = GATHER_ROWS * st
    for s in range(nslab):
        cs = slice(s * LANES, (s + 1) * LANES)
        w = [jnp.broadcast_to(w_ref[k:k + 1, cs], (GATHER_ROWS, LANES)) for k in range(taps)]
        bias = jnp.broadcast_to(b_ref[0:1, cs], (GATHER_ROWS, LANES))
        for g in range(rows // span):
            acc = [bias] * st
            for m in range(st + taps - 1):
                uv = src[s, pl.ds(halo + g * span + m - (taps - 1), GATHER_ROWS, stride=st), :]
                for a in range(st):
                    if 0 <= m - a < taps:
                        acc[a] = acc[a] + w[m - a] * uv
            for a in range(st):
                dst[s, pl.ds(g * span + a, GATHER_ROWS, stride=st), :] = acc[a]


def _from_slabs(buf, lo, hi):
    return jnp.concatenate([buf[s] for s in range(lo, hi)], axis=-1)


def _inproj_kernel(x_ref, g_ref, wa_ref, wb_ref, wif_ref, qw_ref, qb_ref, cw_ref, cb_ref,
                   lg_ref, lb_ref,
                   q_ref, k_ref, v_ref, so_ref, uc_ref, gate_ref, if_ref,
                   qbuf, ubuf, ybuf_q, ybuf_c, *, tm, d, tiles_per_seq, q_scale):
    i = pl.program_id(0)
    nsl = d // LANES

    @pl.when(i == 0)
    def _():
        qbuf[:, pl.ds(tm, QK_HALO), :] = jnp.zeros((2 * nsl, QK_HALO, LANES), F32)
        ubuf[:, pl.ds(tm, CONF_HALO), :] = jnp.zeros((nsl, CONF_HALO, LANES), F32)

    first = i % tiles_per_seq == 0
    h = _rms(x_ref[...], g_ref[...]).astype(BF16)

    pglu = jnp.dot(h, wb_ref[:, 0:2 * d], preferred_element_type=F32)
    _fill_slabs(ubuf, pglu[:, :d] * _sigmoid(pglu[:, d:]), first, halo=CONF_HALO, rows=tm)
    _fill_slabs(qbuf, jnp.dot(h, wa_ref[:, 0:2 * d], preferred_element_type=F32), first,
                halo=QK_HALO, rows=tm)

    _causal_conv(ubuf, ybuf_c, cw_ref, cb_ref, nslab=nsl, rows=tm, halo=CONF_HALO)
    y = _from_slabs(ybuf_c, 0, nsl)
    yc = y - jnp.mean(y, axis=-1, keepdims=True)
    var = jnp.mean(yc * yc, axis=-1, keepdims=True)
    uc_ref[...] = _silu(yc * lax.rsqrt(var + EPS) * lg_ref[...] + lb_ref[...]).astype(BF16)

    _causal_conv(qbuf, ybuf_q, qw_ref, qb_ref, nslab=2 * nsl, rows=tm, halo=QK_HALO)
    q_ref[...] = (_silu(_from_slabs(ybuf_q, 0, nsl)) * q_scale).astype(BF16)
    k_ref[...] = _silu(_from_slabs(ybuf_q, nsl, 2 * nsl)).astype(BF16)

    pvo = jnp.dot(h, wa_ref[:, 2 * d:4 * d], preferred_element_type=F32)
    v_ref[...] = pvo[:, :d].astype(BF16)
    so_ref[...] = _sigmoid(pvo[:, d:]).astype(BF16)

    pg = jnp.dot(h, wb_ref[:, 2 * d:4 * d], preferred_element_type=F32)
    gate_ref[...] = _sigmoid(pg).astype(BF16)

    if_ref[...] = lax.dot_general(wif_ref[...], h, (((1,), (1,)), ((), ())),
                                  preferred_element_type=F32)


def _inproj(x2, g, wa, wb, wif_t, qw, qb, cw, cb, lg, lb, *, seq, tm):
    t, d = x2.shape
    nsl = d // LANES
    kern = functools.partial(_inproj_kernel, tm=tm, d=d, tiles_per_seq=seq // tm,
                             q_scale=float((d // M_HEADS) ** -0.5))
    tok = lambda width: pl.BlockSpec((tm, width), lambda i: (i, 0))
    out_shape = [jax.ShapeDtypeStruct((t, d), BF16)] * 5 + [
        jax.ShapeDtypeStruct((t, 2 * d), BF16),
        jax.ShapeDtypeStruct((wif_t.shape[0], t), F32)]
    consts = (g, wa, wb, wif_t, qw, qb, cw, cb, lg, lb)
    return pl.pallas_call(
        kern,
        grid=(t // tm,),
        in_specs=[tok(d)] + [_resident(a.shape) for a in consts],
        out_specs=[tok(d)] * 5 + [tok(2 * d),
                                  pl.BlockSpec((wif_t.shape[0], tm), lambda i: (0, i))],
        out_shape=out_shape,
        scratch_shapes=[pltpu.VMEM((2 * nsl, tm + QK_HALO, LANES), F32),
                        pltpu.VMEM((nsl, tm + CONF_HALO, LANES), F32),
                        pltpu.VMEM((2 * nsl, tm, LANES), F32),
                        pltpu.VMEM((nsl, tm, LANES), F32)],
        compiler_params=pltpu.CompilerParams(
            dimension_semantics=("arbitrary",), vmem_limit_bytes=VMEM_LIMIT),
        name="inproj",
    )(x2, *consts)


def _gates_kernel(z_ref, bif_ref, o_ref):
    nh = M_HEADS
    pre = z_ref[...] + bif_ref[...]
    lf = jnp.minimum(pre, 0.0) - jnp.log(1.0 + jnp.exp(-jnp.abs(pre)))
    lane_in_chunk = lax.broadcasted_iota(jnp.int32, pre.shape, 1) % MLSTM_CHUNK
    b = lf
    sh = 1
    while sh < MLSTM_CHUNK:
        b = b + jnp.where(lane_in_chunk >= sh, pltpu.roll(b, sh, axis=1), 0.0)
        sh *= 2
    is_i = lax.broadcasted_iota(jnp.int32, pre.shape, 0) < nh
    o_ref[...] = jnp.where(is_i, pre, b)


def _gates(gz, bif):
    full = lambda a: pl.BlockSpec(a.shape, lambda i: (0,) * a.ndim)
    return pl.pallas_call(
        _gates_kernel,
        grid=(1,),
        in_specs=[full(gz), full(bif)],
        out_specs=full(gz),
        out_shape=jax.ShapeDtypeStruct(gz.shape, F32),
        name="gates",
    )(gz, bif)


def _mlstm_kernel(q_ref, k_ref, v_ref, so_ref, gz_ref, ng_ref, o_ref,
                  c_ref, m_ref, *, ts, dh):
    L = MLSTM_CHUNK
    nh = M_HEADS
    j = pl.program_id(1)

    @pl.when(j == 0)
    def _():
        c_ref[...] = jnp.zeros(c_ref.shape, F32)
        m_ref[...] = jnp.zeros(m_ref.shape, F32)

    row = lax.broadcasted_iota(jnp.int32, (L, L), 0)
    col = lax.broadcasted_iota(jnp.int32, (L, L), 1)
    causal = col <= row
    eye = col == row
    ones_col = (lax.broadcasted_iota(jnp.int32, (L, LANES), 1) == 0).astype(BF16)

    gates = gz_ref[...]

    m_state = [m_ref[hh:hh + 1, 0:1] for hh in range(nh)]
    for ci in range(ts // L):
        r0 = ci * L
        for hh in range(nh):
            cs = slice(hh * dh, (hh + 1) * dh)
            b_row = gates[nh + hh:nh + hh + 1, r0:r0 + L]
            a_row = gates[hh:hh + 1, r0:r0 + L] - b_row
            a_mat = jnp.broadcast_to(a_row, (L, L))
            m_prev = m_state[hh]
            c_col = jnp.maximum(
                jnp.max(jnp.where(causal, a_mat, -jnp.inf), axis=-1, keepdims=True), m_prev)
            b_col = jnp.sum(jnp.where(eye, jnp.broadcast_to(b_row, (L, L)), 0.0),
                            axis=-1, keepdims=True)
            p_mat = jnp.exp(jnp.where(causal, a_mat - c_col, -jnp.inf))
            w_inter = jnp.exp(m_prev - c_col)

            qh = q_ref[r0:r0 + L, cs]
            kh = k_ref[r0:r0 + L, cs]
            v_aug = jnp.concatenate([v_ref[r0:r0 + L, cs], ones_col], axis=-1)
            s = lax.dot_general(qh, kh, (((1,), (1,)), ((), ())), preferred_element_type=F32)
            sg = (s * p_mat).astype(BF16)
            c_aug = c_ref[hh]
            r = (jnp.dot(sg, v_aug, preferred_element_type=F32)
                 + w_inter * jnp.dot(qh, c_aug.astype(BF16), preferred_element_type=F32))
            den = r[:, dh:dh + 1]
            m_t = b_col + c_col
            hv = r[:, :dh] * (1.0 / jnp.maximum(jnp.abs(den), jnp.exp(-m_t)))
            hn = _rms(hv, ng_ref[:, cs])
            o_ref[r0:r0 + L, cs] = (hn * so_ref[r0:r0 + L, cs].astype(F32)).astype(BF16)

            c_last = c_col[L - 1:L, :]
            wk_row = jnp.exp(a_row - c_last)
            decay = jnp.exp(m_prev - c_last)
            kt = (kh.astype(F32).T * wk_row).astype(BF16)
            c_ref[hh] = decay * c_aug + jnp.dot(kt, v_aug, preferred_element_type=F32)
            m_state[hh] = b_row[:, L - 1:L] + c_last
    for hh in range(nh):
        m_ref[hh:hh + 1, :] = jnp.broadcast_to(m_state[hh], (1, LANES))


def _mlstm(q, k, v, so, gz, ng, *, batch, seq, ts):
    t, dm = q.shape
    dh = dm // M_HEADS
    ns = seq // ts
    tok = pl.BlockSpec((ts, dm), lambda b, j: (b * ns + j, 0))
    kern = functools.partial(_mlstm_kernel, ts=ts, dh=dh)
    return pl.pallas_call(
        kern,
        grid=(batch, ns),
        in_specs=[tok, tok, tok, tok,
                  pl.BlockSpec((2 * M_HEADS, ts), lambda b, j: (0, b * ns + j)),
                  _resident(ng.shape)],
        out_specs=tok,
        out_shape=jax.ShapeDtypeStruct((t, dm), BF16),
        scratch_shapes=[pltpu.VMEM((M_HEADS, dh, dh + LANES), F32),
                        pltpu.VMEM((SUBLANES, LANES), F32)],
        compiler_params=pltpu.CompilerParams(
            dimension_semantics=("arbitrary", "arbitrary"), vmem_limit_bytes=VMEM_LIMIT),
        name="mlstm",
    )(q, k, v, so, gz, ng)


def _merge_kernel(x_ref, hm_ref, uc_ref, gate_ref, wm_ref, wc_ref, wo_ref, o_ref, *, d):
    bm = jnp.dot(hm_ref[...], wm_ref[...], preferred_element_type=F32)
    bc = jnp.dot(uc_ref[...], wc_ref[...], preferred_element_type=F32)
    mix = gate_ref[:, :d].astype(F32) * bm + gate_ref[:, d:].astype(F32) * bc
    o_ref[...] = x_ref[...] + jnp.dot(mix.astype(BF16), wo_ref[...], preferred_element_type=F32)


def _merge(x2, hm, uc, gates, wm, wc, wo, *, tm):
    t, d = x2.shape
    tok = lambda width: pl.BlockSpec((tm, width), lambda i: (i, 0))
    return pl.pallas_call(
        functools.partial(_merge_kernel, d=d),
        grid=(t // tm,),
        in_specs=[tok(d), tok(d), tok(d), tok(2 * d)]
        + [_resident(a.shape) for a in (wm, wc, wo)],
        out_specs=tok(d),
        out_shape=jax.ShapeDtypeStruct((t, d), F32),
        compiler_params=pltpu.CompilerParams(
            dimension_semantics=("arbitrary",), vmem_limit_bytes=VMEM_LIMIT),
        name="merge",
    )(x2, hm, uc, gates, wm, wc, wo)


def _tail_kernel(x_ref, p_ref, gf_ref, wg_ref, wu_ref, wd_ref, gp_ref, wpg_ref, wpp_ref,
                 fg_ref, o_ref, *, ff_splits, final_norm):
    x1 = x_ref[...]
    f = _rms(x1, gf_ref[...]).astype(BF16)
    dff = wg_ref.shape[1]
    step = dff // ff_splits
    acc = x1
    for s in range(ff_splits):
        sl = slice(s * step, (s + 1) * step)
        gt = jnp.dot(f, wg_ref[:, sl], preferred_element_type=F32)
        up = jnp.dot(f, wu_ref[:, sl], preferred_element_type=F32)
        hid = (_silu(gt) * up).astype(BF16)
        acc = acc + jnp.dot(hid, wd_ref[sl, :], preferred_element_type=F32)
    x2 = acc
    gate = _sigmoid(jnp.dot(_rms(x2, gp_ref[...]).astype(BF16), wpg_ref[...],
                            preferred_element_type=F32))
    x3 = x2 + gate * jnp.dot(p_ref[...].astype(BF16), wpp_ref[...], preferred_element_type=F32)
    o_ref[...] = _rms(x3, fg_ref[...]) if final_norm else x3


def _tail(x1, p2, gf, wg, wu, wd, gp, wpg, wpp, fg, *, tm, final_norm):
    t, d = x1.shape
    dff = wg.shape[1]
    ff_splits = 2 if (dff // 2) % LANES == 0 else 1
    kern = functools.partial(_tail_kernel, ff_splits=ff_splits, final_norm=final_norm)
    tok = lambda width: pl.BlockSpec((tm, width), lambda i: (i, 0))
    return pl.pallas_call(
        kern,
        grid=(t // tm,),
        in_specs=[tok(d), tok(p2.shape[1])]
        + [_resident(a.shape) for a in (gf, wg, wu, wd, gp, wpg, wpp, fg)],
        out_specs=tok(d),
        out_shape=jax.ShapeDtypeStruct((t, d), F32),
        compiler_params=pltpu.CompilerParams(
            dimension_semantics=("arbitrary",), vmem_limit_bytes=VMEM_LIMIT),
        name="tail",
    )(x1, p2, gf, wg, wu, wd, gp, wpg, wpp, fg)


def kernel(x, p, norm_mix_g, w_in, b_if, conv_qk_w, conv_qk_b, mh_norm_g, conf_conv_w, conf_conv_b, conf_ln_g, conf_ln_b, w_branch_m, w_branch_c, w_out, norm_ffn_g, w_ffn_gate, w_ffn_up, w_ffn_down, norm_ple_g, w_ple_gate, w_ple_proj, final_g):
    batch, seq, d = x.shape
    depth = w_in.shape[0]
    t = batch * seq
    nh = M_HEADS
    n_gate = 2 * nh
    tm = 512
    assert seq % tm == 0 and tm % (GATHER_ROWS * GATHER_STRIDE) == 0 and d % (nh * LANES) == 0
    row = lambda a: a.reshape(1, -1).astype(F32)

    xf = x.reshape(t, d)
    for l in range(depth):
        w = w_in[l]
        wa = w[:, :4 * d].astype(BF16)
        wb = w[:, 4 * d + n_gate:].astype(BF16)
        wif_t = w[:, 4 * d:4 * d + n_gate].T.astype(BF16)

        q, k, v, so, uc, gates, gz = _inproj(
            xf, row(norm_mix_g[l]), wa, wb, wif_t, conv_qk_w[l].astype(F32), row(conv_qk_b[l]),
            conf_conv_w[l].astype(F32), row(conf_conv_b[l]), row(conf_ln_g[l]), row(conf_ln_b[l]),
            seq=seq, tm=tm)

        gates_ib = _gates(gz, b_if[l].reshape(n_gate, 1).astype(F32))
        hm = _mlstm(q, k, v, so, gates_ib, row(mh_norm_g[l]), batch=batch, seq=seq, ts=tm)

        x1 = _merge(xf, hm, uc, gates, w_branch_m[l].astype(BF16), w_branch_c[l].astype(BF16),
                    w_out[l].astype(BF16), tm=tm)

        xf = _tail(x1, p[l].reshape(t, -1), row(norm_ffn_g[l]), w_ffn_gate[l].astype(BF16),
                   w_ffn_up[l].astype(BF16), w_ffn_down[l].astype(BF16), row(norm_ple_g[l]),
                   w_ple_gate[l].astype(BF16), w_ple_proj[l].astype(BF16), row(final_g),
                   tm=tm, final_norm=(l == depth - 1))
    return xf.reshape(batch, seq, d)
```

```python
import functools

import jax
import jax.numpy as jnp
from jax import lax
from jax.experimental import pallas as pl
from jax.experimental.pallas import tpu as pltpu

EPS = 1e-6
M_HEADS = 4
MLSTM_CHUNK = 256
SUBLANES = 8
LANES = 128
GATHER_STRIDE = 4
GATHER_ROWS = 16
QK_HALO = 8
CONF_HALO = 32
PROJ_CHUNK = 512
VMEM_LIMIT = 60 * 1024 * 1024

BF16 = jnp.bfloat16
F32 = jnp.float32
NT_DIMS = (((1,), (1,)), ((), ()))


def _sigmoid(z):
    return 0.5 * jnp.tanh(0.5 * z) + 0.5


def _silu(z):
    h = 0.5 * z
    return h * jnp.tanh(h) + h


def _rms(xf, g):
    return xf * lax.rsqrt(jnp.mean(xf * xf, axis=-1, keepdims=True) + EPS) * g


def _resident(shape):
    nd = len(shape)
    return pl.BlockSpec(shape, lambda *_: (0,) * nd, pipeline_mode=pl.Buffered(1))


def _params(*semantics):
    return pltpu.CompilerParams(dimension_semantics=semantics, vmem_limit_bytes=VMEM_LIMIT)


def _fill_slabs(buf, val, first, *, slab0, halo, rows):
    for j in range(val.shape[1] // LANES):
        s = slab0 + j
        prev = buf[s, pl.ds(rows, halo), :]
        buf[s, pl.ds(0, halo), :] = jnp.where(first, 0.0, prev)
        buf[s, pl.ds(halo, rows), :] = val[:, j * LANES:(j + 1) * LANES]


def _causal_conv(src, dst, w_ref, b_ref, *, slabs, rows, halo):
    taps = w_ref.shape[0]
    st = GATHER_STRIDE
    span = GATHER_ROWS * st
    for s in slabs:
        cs = slice(s * LANES, (s + 1) * LANES)
        w = [jnp.broadcast_to(w_ref[k:k + 1, cs], (GATHER_ROWS, LANES)) for k in range(taps)]
        bias = jnp.broadcast_to(b_ref[0:1, cs], (GATHER_ROWS, LANES))
        for g in range(rows // span):
            acc = [bias] * st
            for m in range(st + taps - 1):
                uv = src[s, pl.ds(halo + g * span + m - (taps - 1), GATHER_ROWS, stride=st), :]
                for a in range(st):
                    if 0 <= m - a < taps:
                        acc[a] = acc[a] + w[m - a] * uv
            for a in range(st):
                dst[s, pl.ds(g * span + a, GATHER_ROWS, stride=st), :] = acc[a]


def _from_slabs(buf, lo, hi):
    return jnp.concatenate([buf[s] for s in range(lo, hi)], axis=-1)


def _inproj_kernel(x_ref, g_ref, wqk_ref, wvo_ref, wb_ref, wif_ref, qw_ref, qb_ref, cw_ref,
                   cb_ref, lg_ref, lb_ref,
                   q_ref, k_ref, vt_ref, sot_ref, uc_ref, gate_ref, if_ref,
                   qbuf, ubuf, ybuf_q, ybuf_c, *, tm, d, tiles_per_seq, q_scale):
    i = pl.program_id(0)
    nsl = d // LANES
    pc = PROJ_CHUNK
    spc = pc // LANES

    @pl.when(i == 0)
    def _():
        qbuf[:, pl.ds(tm, QK_HALO), :] = jnp.zeros((2 * nsl, QK_HALO, LANES), F32)
        ubuf[:, pl.ds(tm, CONF_HALO), :] = jnp.zeros((nsl, CONF_HALO, LANES), F32)

    first = i % tiles_per_seq == 0
    h = _rms(x_ref[...], g_ref[...]).astype(BF16)

    def glu(c):
        hw = pc // 2
        a = jnp.dot(h, wb_ref[:, c * hw:(c + 1) * hw], preferred_element_type=F32)
        ga = jnp.dot(h, wb_ref[:, d + c * hw:d + (c + 1) * hw], preferred_element_type=F32)
        _fill_slabs(ubuf, a * _sigmoid(ga), first, slab0=c * (spc // 2), halo=CONF_HALO, rows=tm)

    def conf_conv(c):
        _causal_conv(ubuf, ybuf_c, cw_ref, cb_ref, rows=tm, halo=CONF_HALO,
                     slabs=range(c * (spc // 2), (c + 1) * (spc // 2)))

    def qk(c):
        _fill_slabs(qbuf, jnp.dot(h, wqk_ref[:, c * pc:(c + 1) * pc], preferred_element_type=F32),
                    first, slab0=c * spc, halo=QK_HALO, rows=tm)

    def qk_conv(c):
        _causal_conv(qbuf, ybuf_q, qw_ref, qb_ref, rows=tm, halo=QK_HALO,
                     slabs=range(c * spc, (c + 1) * spc))

    def vo(c):
        r = lax.dot_general(wvo_ref[c * pc:(c + 1) * pc, :], h, NT_DIMS, preferred_element_type=F32)
        if (c + 1) * pc <= d:
            vt_ref[c * pc:(c + 1) * pc, :] = r.astype(BF16)
        else:
            sot_ref[c * pc - d:(c + 1) * pc - d, :] = _sigmoid(r).astype(BF16)

    def gate(c):
        r = jnp.dot(h, wb_ref[:, 2 * d + c * pc:2 * d + (c + 1) * pc], preferred_element_type=F32)
        gate_ref[:, c * pc:(c + 1) * pc] = _sigmoid(r).astype(BF16)

    def conf_norm():
        y = _from_slabs(ybuf_c, 0, nsl)
        yc = y - jnp.mean(y, axis=-1, keepdims=True)
        var = jnp.mean(yc * yc, axis=-1, keepdims=True)
        uc_ref[...] = _silu(yc * lax.rsqrt(var + EPS) * lg_ref[...] + lb_ref[...]).astype(BF16)

    assert 2 * d // pc == 4
    glu(0)
    glu(1); conf_conv(0)
    glu(2); qk(0); conf_conv(1)
    glu(3); qk(1); conf_conv(2)
    qk(2); qk(3); conf_conv(3)
    vo(0); vo(1); qk_conv(0)
    conf_norm(); vo(2); qk_conv(1)
    vo(3); gate(0); qk_conv(2)
    gate(1); qk_conv(3)
    q_ref[...] = (_silu(_from_slabs(ybuf_q, 0, nsl)) * q_scale).astype(BF16)
    gate(2)
    k_ref[...] = _silu(_from_slabs(ybuf_q, nsl, 2 * nsl)).astype(BF16)
    gate(3)
    if_ref[...] = lax.dot_general(wif_ref[...], h, NT_DIMS, preferred_element_type=F32)


def _inproj(x2, g, wqk, wvo_t, wb, wif_t, qw, qb, cw, cb, lg, lb, *, seq, tm):
    t, d = x2.shape
    nsl = d // LANES
    kern = functools.partial(_inproj_kernel, tm=tm, d=d, tiles_per_seq=seq // tm,
                             q_scale=float((d // M_HEADS) ** -0.5))
    tok = lambda width: pl.BlockSpec((tm, width), lambda i: (i, 0))
    feat = lambda rows: pl.BlockSpec((rows, tm), lambda i: (0, i))
    n_gate = wif_t.shape[0]
    out_shape = [jax.ShapeDtypeStruct((t, d), BF16), jax.ShapeDtypeStruct((t, d), BF16),
                 jax.ShapeDtypeStruct((d, t), BF16), jax.ShapeDtypeStruct((d, t), BF16),
                 jax.ShapeDtypeStruct((t, d), BF16), jax.ShapeDtypeStruct((t, 2 * d), BF16),
                 jax.ShapeDtypeStruct((n_gate, t), F32)]
    consts = (g, wqk, wvo_t, wb, wif_t, qw, qb, cw, cb, lg, lb)
    return pl.pallas_call(
        kern,
        grid=(t // tm,),
        in_specs=[tok(d)] + [_resident(a.shape) for a in consts],
        out_specs=[tok(d), tok(d), feat(d), feat(d), tok(d), tok(2 * d), feat(n_gate)],
        out_shape=out_shape,
        scratch_shapes=[pltpu.VMEM((2 * nsl, tm + QK_HALO, LANES), F32),
                        pltpu.VMEM((nsl, tm + CONF_HALO, LANES), F32),
                        pltpu.VMEM((2 * nsl, tm, LANES), F32),
                        pltpu.VMEM((nsl, tm, LANES), F32)],
        compiler_params=_params("arbitrary"),
        name="inproj",
    )(x2, *consts)


def _chunk_scan(x, op, identity):
    lane_in_chunk = lax.broadcasted_iota(jnp.int32, x.shape, 1) % MLSTM_CHUNK
    sh = 1
    while sh < MLSTM_CHUNK:
        x = op(x, jnp.where(lane_in_chunk >= sh, pltpu.roll(x, sh, axis=1), identity))
        sh *= 2
    return x


def _gates_kernel(z_ref, bif_ref, ab_ref, cm_ref):
    nh = M_HEADS
    pre = z_ref[...] + bif_ref[...]
    lf = jnp.minimum(pre, 0.0) - jnp.log(1.0 + jnp.exp(-jnp.abs(pre)))
    b = _chunk_scan(lf, jnp.add, 0.0)
    a = pre - pltpu.roll(b, nh, axis=0)
    is_i = lax.broadcasted_iota(jnp.int32, pre.shape, 0) < nh
    ab_ref[...] = jnp.where(is_i, a, b)
    cm_ref[...] = _chunk_scan(a, jnp.maximum, -jnp.inf)


def _gates(gz, bif):
    full = lambda a: pl.BlockSpec(a.shape, lambda i: (0,) * a.ndim)
    out = jax.ShapeDtypeStruct(gz.shape, F32)
    return pl.pallas_call(
        _gates_kernel,
        grid=(1,),
        in_specs=[full(gz), full(bif)],
        out_specs=[full(gz), full(gz)],
        out_shape=[out, out],
        name="gates",
    )(gz, bif)


def _mlstm_kernel(q_ref, k_ref, vt_ref, sot_ref, ab_ref, cm_ref, acol_ref, ng_ref, o_ref,
                  ct_ref, m_ref, *, ts, dh):
    L = MLSTM_CHUNK
    nh = M_HEADS
    j = pl.program_id(1)

    @pl.when(j == 0)
    def _():
        ct_ref[...] = jnp.zeros(ct_ref.shape, F32)
        m_ref[...] = jnp.zeros(m_ref.shape, F32)

    src = lax.broadcasted_iota(jnp.int32, (L, L), 0)
    tgt = lax.broadcasted_iota(jnp.int32, (L, L), 1)
    causal = src <= tgt
    ones_row = (lax.broadcasted_iota(jnp.int32, (LANES, L), 0) == 0).astype(BF16)

    m_state = [m_ref[hh:hh + 1, 0:1] for hh in range(nh)]
    for ci in range(ts // L):
        ts_ = slice(ci * L, (ci + 1) * L)
        for hh in range(nh):
            fs = slice(hh * dh, (hh + 1) * dh)
            a_row = ab_ref[hh:hh + 1, ts_]
            b_row = ab_ref[nh + hh:nh + hh + 1, ts_]
            m_prev = m_state[hh]
            c_row = jnp.maximum(cm_ref[hh:hh + 1, ts_], m_prev)
            a_col = acol_ref[ts_, hh:hh + 1]
            p_t = jnp.exp(jnp.where(causal, a_col - c_row, -jnp.inf))
            w_inter = jnp.exp(m_prev - c_row)

            qh = q_ref[ts_, fs]
            kh = k_ref[ts_, fs]
            vt_aug = jnp.concatenate([vt_ref[fs, ts_], ones_row], axis=0)
            ct = ct_ref[hh]
            both = lax.dot_general(jnp.concatenate([kh, ct.astype(BF16)], axis=0), qh, NT_DIMS,
                                   preferred_element_type=F32)
            sg_t = (both[:L] * p_t).astype(BF16)
            r_t = jnp.dot(vt_aug, sg_t, preferred_element_type=F32) + w_inter * both[L:]
            den = r_t[dh:dh + 1, :]
            m_t = b_row + c_row
            hv = r_t[:dh] * (1.0 / jnp.maximum(jnp.abs(den), jnp.exp(-m_t)))
            scale = lax.rsqrt(jnp.mean(hv * hv, axis=0, keepdims=True) + EPS)
            out_t = hv * scale * ng_ref[fs, :] * sot_ref[fs, ts_].astype(F32)
            o_ref[ts_, fs] = out_t.T.astype(BF16)

            c_last = c_row[:, L - 1:L]
            wk_row = jnp.exp(a_row - c_last)
            decay = jnp.exp(m_prev - c_last)
            vw = (vt_aug.astype(F32) * wk_row).astype(BF16)
            ct_ref[hh] = decay * ct + jnp.dot(vw, kh, preferred_element_type=F32)
            m_state[hh] = b_row[:, L - 1:L] + c_last
    for hh in range(nh):
        m_ref[hh:hh + 1, :] = jnp.broadcast_to(m_state[hh], (1, LANES))


def _mlstm(q, k, vt, sot, ab, cm, acol, ng_mat, *, batch, seq, ts):
    t, dm = q.shape
    dh = dm // M_HEADS
    ns = seq // ts
    tok = lambda width: pl.BlockSpec((ts, width), lambda b, j: (b * ns + j, 0))
    feat = lambda rows: pl.BlockSpec((rows, ts), lambda b, j: (0, b * ns + j))
    kern = functools.partial(_mlstm_kernel, ts=ts, dh=dh)
    return pl.pallas_call(
        kern,
        grid=(batch, ns),
        in_specs=[tok(dm), tok(dm), feat(dm), feat(dm), feat(ab.shape[0]), feat(cm.shape[0]),
                  tok(acol.shape[1]), _resident(ng_mat.shape)],
        out_specs=tok(dm),
        out_shape=jax.ShapeDtypeStruct((t, dm), BF16),
        scratch_shapes=[pltpu.VMEM((M_HEADS, dh + LANES, dh), F32),
                        pltpu.VMEM((SUBLANES, LANES), F32)],
        compiler_params=_params("arbitrary", "arbitrary"),
        name="mlstm",
    )(q, k, vt, sot, ab, cm, acol, ng_mat)


def _merge_kernel(x_ref, hm_ref, uc_ref, gate_ref, wm_ref, wc_ref, wo_ref, o_ref, *, d):
    bm = jnp.dot(hm_ref[...], wm_ref[...], preferred_element_type=F32)
    bc = jnp.dot(uc_ref[...], wc_ref[...], preferred_element_type=F32)
    mix = gate_ref[:, :d].astype(F32) * bm + gate_ref[:, d:].astype(F32) * bc
    o_ref[...] = x_ref[...] + jnp.dot(mix.astype(BF16), wo_ref[...], preferred_element_type=F32)


def _merge(x2, hm, uc, gates, wm, wc, wo, *, tm):
    t, d = x2.shape
    tok = lambda width: pl.BlockSpec((tm, width), lambda i: (i, 0))
    return pl.pallas_call(
        functools.partial(_merge_kernel, d=d),
        grid=(t // tm,),
        in_specs=[tok(d), tok(d), tok(d), tok(2 * d)]
        + [_resident(a.shape) for a in (wm, wc, wo)],
        out_specs=tok(d),
        out_shape=jax.ShapeDtypeStruct((t, d), F32),
        compiler_params=_params("arbitrary"),
        name="merge",
    )(x2, hm, uc, gates, wm, wc, wo)


def _tail_kernel(x_ref, p_ref, gf_ref, wg_ref, wu_ref, wd_ref, gp_ref, wpg_ref, wpp_ref,
                 fg_ref, o_ref, *, ff_splits, final_norm):
    x1 = x_ref[...]
    f = _rms(x1, gf_ref[...]).astype(BF16)
    dff = wg_ref.shape[1]
    step = dff // ff_splits
    acc = x1
    for s in range(ff_splits):
        sl = slice(s * step, (s + 1) * step)
        gt = jnp.dot(f, wg_ref[:, sl], preferred_element_type=F32)
        up = jnp.dot(f, wu_ref[:, sl], preferred_element_type=F32)
        hid = (_silu(gt) * up).astype(BF16)
        acc = acc + jnp.dot(hid, wd_ref[sl, :], preferred_element_type=F32)
    x2 = acc
    gate = _sigmoid(jnp.dot(_rms(x2, gp_ref[...]).astype(BF16), wpg_ref[...],
                            preferred_element_type=F32))
    x3 = x2 + gate * jnp.dot(p_ref[...].astype(BF16), wpp_ref[...], preferred_element_type=F32)
    o_ref[...] = _rms(x3, fg_ref[...]) if final_norm else x3


def _tail(x1, p2, gf, wg, wu, wd, gp, wpg, wpp, fg, *, tm, final_norm):
    t, d = x1.shape
    dff = wg.shape[1]
    ff_splits = 2 if (dff // 2) % LANES == 0 else 1
    kern = functools.partial(_tail_kernel, ff_splits=ff_splits, final_norm=final_norm)
    tok = lambda width: pl.BlockSpec((tm, width), lambda i: (i, 0))
    return pl.pallas_call(
        kern,
        grid=(t // tm,),
        in_specs=[tok(d), tok(p2.shape[1])]
        + [_resident(a.shape) for a in (gf, wg, wu, wd, gp, wpg, wpp, fg)],
        out_specs=tok(d),
        out_shape=jax.ShapeDtypeStruct((t, d), F32),
        compiler_params=_params("arbitrary"),
        name="tail",
    )(x1, p2, gf, wg, wu, wd, gp, wpg, wpp, fg)


def kernel(x, p, norm_mix_g, w_in, b_if, conv_qk_w, conv_qk_b, mh_norm_g, conf_conv_w, conf_conv_b, conf_ln_g, conf_ln_b, w_branch_m, w_branch_c, w_out, norm_ffn_g, w_ffn_gate, w_ffn_up, w_ffn_down, norm_ple_g, w_ple_gate, w_ple_proj, final_g):
    batch, seq, d = x.shape
    depth = w_in.shape[0]
    t = batch * seq
    nh = M_HEADS
    n_gate = 2 * nh
    tm = 512
    assert seq % tm == 0 and tm % (GATHER_ROWS * GATHER_STRIDE) == 0 and tm % MLSTM_CHUNK == 0
    assert d % (nh * LANES) == 0 and 2 * d == 4 * PROJ_CHUNK
    row = lambda a: a.reshape(1, -1).astype(F32)

    xf = x.reshape(t, d)
    for l in range(depth):
        w = w_in[l]
        wqk = w[:, :2 * d].astype(BF16)
        wvo_t = w[:, 2 * d:4 * d].T.astype(BF16)
        wb = w[:, 4 * d + n_gate:].astype(BF16)
        wif_t = w[:, 4 * d:4 * d + n_gate].T.astype(BF16)

        q, k, vt, sot, uc, gates, gz = _inproj(
            xf, row(norm_mix_g[l]), wqk, wvo_t, wb, wif_t, conv_qk_w[l].astype(F32),
            row(conv_qk_b[l]), conf_conv_w[l].astype(F32), row(conf_conv_b[l]),
            row(conf_ln_g[l]), row(conf_ln_b[l]), seq=seq, tm=tm)

        ab, cm = _gates(gz, b_if[l].reshape(n_gate, 1).astype(F32))
        ng_mat = jnp.broadcast_to(mh_norm_g[l].astype(F32).reshape(d, 1), (d, MLSTM_CHUNK))
        hm = _mlstm(q, k, vt, sot, ab, cm, ab.T, ng_mat, batch=batch, seq=seq, ts=tm)

        x1 = _merge(xf, hm, uc, gates, w_branch_m[l].astype(BF16), w_branch_c[l].astype(BF16),
                    w_out[l].astype(BF16), tm=tm)

        xf = _tail(x1, p[l].reshape(t, -1), row(norm_ffn_g[l]), w_ffn_gate[l].astype(BF16),
                   w_ffn_up[l].astype(BF16), w_ffn_down[l].astype(BF16), row(norm_ple_g[l]),
                   w_ple_gate[l].astype(BF16), w_ple_proj[l].astype(BF16), row(final_g),
                   tm=tm, final_norm=(l == depth - 1))
    return xf.reshape(batch, seq, d)
```

```python
import functools

import jax
import jax.numpy as jnp
from jax import lax
from jax.experimental import pallas as pl
from jax.experimental.pallas import tpu as pltpu

EPS = 1e-6
M_HEADS = 4
MLSTM_CHUNK = 256
SUBLANES = 8
LANES = 128
MXU_TILE = 256
GATHER_STRIDE = 4
GATHER_ROWS = 16
QK_HALO = 8
CONF_HALO = 32
PROJ_CHUNK = 512
VMEM_LIMIT = 60 * 1024 * 1024

BF16 = jnp.bfloat16
F32 = jnp.float32
NT_DIMS = (((1,), (1,)), ((), ()))


def _sigmoid(z):
    return 0.5 * jnp.tanh(0.5 * z) + 0.5


def _silu(z):
    h = 0.5 * z
    return h * jnp.tanh(h) + h


def _sigmoid_bf16(z):
    return _sigmoid(z.astype(BF16))


def _silu_bf16(z):
    return _silu(z.astype(BF16))


def _rms(xf, g):
    return xf * lax.rsqrt(jnp.mean(xf * xf, axis=-1, keepdims=True) + EPS) * g


def _resident(shape):
    nd = len(shape)
    return pl.BlockSpec(shape, lambda *_: (0,) * nd, pipeline_mode=pl.Buffered(1))


def _params(*semantics):
    return pltpu.CompilerParams(dimension_semantics=semantics, vmem_limit_bytes=VMEM_LIMIT)


def _fill_slabs(buf, val, first, *, slab0, halo, rows):
    for j in range(val.shape[1] // LANES):
        s = slab0 + j
        prev = buf[s, pl.ds(rows, halo), :]
        buf[s, pl.ds(0, halo), :] = jnp.where(first, 0.0, prev)
        buf[s, pl.ds(halo, rows), :] = val[:, j * LANES:(j + 1) * LANES]


def _causal_conv(src, dst, w_ref, b_ref, *, slabs, rows, halo):
    taps = w_ref.shape[0]
    st = GATHER_STRIDE
    span = GATHER_ROWS * st
    for s in slabs:
        cs = slice(s * LANES, (s + 1) * LANES)
        w = [jnp.broadcast_to(w_ref[k:k + 1, cs], (GATHER_ROWS, LANES)) for k in range(taps)]
        bias = jnp.broadcast_to(b_ref[0:1, cs], (GATHER_ROWS, LANES))
        for g in range(rows // span):
            acc = [bias] * st
            for m in range(st + taps - 1):
                uv = src[s, pl.ds(halo + g * span + m - (taps - 1), GATHER_ROWS, stride=st), :]
                for a in range(st):
                    if 0 <= m - a < taps:
                        acc[a] = acc[a] + w[m - a] * uv
            for a in range(st):
                dst[s, pl.ds(g * span + a, GATHER_ROWS, stride=st), :] = acc[a]


def _from_slabs(buf, lo, hi):
    return jnp.concatenate([buf[s] for s in range(lo, hi)], axis=-1)


def _inproj_kernel(x_ref, g_ref, wqk_ref, wvo_ref, wb_ref, wif_ref, qw_ref, qb_ref, cw_ref,
                   cb_ref, lg_ref, lb_ref,
                   q_ref, k_ref, vt_ref, sot_ref, uc_ref, gate_ref, if_ref,
                   qbuf, ubuf, ybuf_q, ybuf_c, *, tm, d, tiles_per_seq, q_scale):
    i = pl.program_id(0)
    nsl = d // LANES
    pc = PROJ_CHUNK
    spc = pc // LANES

    @pl.when(i == 0)
    def _():
        qbuf[:, pl.ds(tm, QK_HALO), :] = jnp.zeros((2 * nsl, QK_HALO, LANES), F32)
        ubuf[:, pl.ds(tm, CONF_HALO), :] = jnp.zeros((nsl, CONF_HALO, LANES), F32)

    first = i % tiles_per_seq == 0
    h = _rms(x_ref[...], g_ref[...]).astype(BF16)

    def glu(c):
        hw = pc // 2
        a = jnp.dot(h, wb_ref[:, c * hw:(c + 1) * hw], preferred_element_type=F32)
        ga = jnp.dot(h, wb_ref[:, d + c * hw:d + (c + 1) * hw], preferred_element_type=F32)
        _fill_slabs(ubuf, a * _sigmoid(ga), first, slab0=c * (spc // 2), halo=CONF_HALO, rows=tm)

    def conf_conv(c):
        _causal_conv(ubuf, ybuf_c, cw_ref, cb_ref, rows=tm, halo=CONF_HALO,
                     slabs=range(c * (spc // 2), (c + 1) * (spc // 2)))

    def qk(c):
        _fill_slabs(qbuf, jnp.dot(h, wqk_ref[:, c * pc:(c + 1) * pc], preferred_element_type=F32),
                    first, slab0=c * spc, halo=QK_HALO, rows=tm)

    def qk_conv(c):
        _causal_conv(qbuf, ybuf_q, qw_ref, qb_ref, rows=tm, halo=QK_HALO,
                     slabs=range(c * spc, (c + 1) * spc))

    def vo(c):
        r = lax.dot_general(wvo_ref[c * pc:(c + 1) * pc, :], h, NT_DIMS, preferred_element_type=F32)
        if (c + 1) * pc <= d:
            vt_ref[c * pc:(c + 1) * pc, :] = r.astype(BF16)
        else:
            sot_ref[c * pc - d:(c + 1) * pc - d, :] = _sigmoid_bf16(r)

    def gate(c):
        r = jnp.dot(h, wb_ref[:, 2 * d + c * pc:2 * d + (c + 1) * pc], preferred_element_type=F32)
        gate_ref[:, c * pc:(c + 1) * pc] = _sigmoid_bf16(r)

    def conf_norm():
        y = _from_slabs(ybuf_c, 0, nsl)
        yc = y - jnp.mean(y, axis=-1, keepdims=True)
        var = jnp.mean(yc * yc, axis=-1, keepdims=True)
        uc_ref[...] = _silu_bf16(yc * lax.rsqrt(var + EPS) * lg_ref[...] + lb_ref[...])

    assert 2 * d // pc == 4
    glu(0)
    glu(1); conf_conv(0)
    glu(2); qk(0); conf_conv(1)
    glu(3); qk(1); conf_conv(2)
    qk(2); qk(3); conf_conv(3)
    vo(0); vo(1); qk_conv(0)
    conf_norm(); vo(2); qk_conv(1)
    vo(3); gate(0); qk_conv(2)
    gate(1); qk_conv(3)
    q_ref[...] = _silu_bf16(_from_slabs(ybuf_q, 0, nsl)) * q_scale
    gate(2)
    k_ref[...] = _silu_bf16(_from_slabs(ybuf_q, nsl, 2 * nsl))
    gate(3)
    if_ref[...] = lax.dot_general(wif_ref[...], h, NT_DIMS, preferred_element_type=F32)


def _inproj(x2, g, wqk, wvo_t, wb, wif_t, qw, qb, cw, cb, lg, lb, *, seq, tm):
    t, d = x2.shape
    nsl = d // LANES
    kern = functools.partial(_inproj_kernel, tm=tm, d=d, tiles_per_seq=seq // tm,
                             q_scale=float((d // M_HEADS) ** -0.5))
    tok = lambda width: pl.BlockSpec((tm, width), lambda i: (i, 0))
    feat = lambda rows: pl.BlockSpec((rows, tm), lambda i: (0, i))
    n_gate = wif_t.shape[0]
    out_shape = [jax.ShapeDtypeStruct((t, d), BF16), jax.ShapeDtypeStruct((t, d), BF16),
                 jax.ShapeDtypeStruct((d, t), BF16), jax.ShapeDtypeStruct((d, t), BF16),
                 jax.ShapeDtypeStruct((t, d), BF16), jax.ShapeDtypeStruct((t, 2 * d), BF16),
                 jax.ShapeDtypeStruct((n_gate, t), F32)]
    consts = (g, wqk, wvo_t, wb, wif_t, qw, qb, cw, cb, lg, lb)
    return pl.pallas_call(
        kern,
        grid=(t // tm,),
        in_specs=[tok(d)] + [_resident(a.shape) for a in consts],
        out_specs=[tok(d), tok(d), feat(d), feat(d), tok(d), tok(2 * d), feat(n_gate)],
        out_shape=out_shape,
        scratch_shapes=[pltpu.VMEM((2 * nsl, tm + QK_HALO, LANES), F32),
                        pltpu.VMEM((nsl, tm + CONF_HALO, LANES), F32),
                        pltpu.VMEM((2 * nsl, tm, LANES), F32),
                        pltpu.VMEM((nsl, tm, LANES), F32)],
        compiler_params=_params("arbitrary"),
        name="inproj",
    )(x2, *consts)


def _chunk_scan(x, op, identity):
    lane_in_chunk = lax.broadcasted_iota(jnp.int32, x.shape, 1) % MLSTM_CHUNK
    sh = 1
    while sh < MLSTM_CHUNK:
        x = op(x, jnp.where(lane_in_chunk >= sh, pltpu.roll(x, sh, axis=1), identity))
        sh *= 2
    return x


def _gates_kernel(z_ref, bif_ref, ab_ref, cm_ref):
    nh = M_HEADS
    pre = z_ref[...] + bif_ref[...]
    lf = jnp.minimum(pre, 0.0) - jnp.log(1.0 + jnp.exp(-jnp.abs(pre)))
    b = _chunk_scan(lf, jnp.add, 0.0)
    a = pre - pltpu.roll(b, nh, axis=0)
    is_i = lax.broadcasted_iota(jnp.int32, pre.shape, 0) < nh
    ab_ref[...] = jnp.where(is_i, a, b)
    cm_ref[...] = _chunk_scan(a, jnp.maximum, -jnp.inf)


def _gates(gz, bif):
    full = lambda a: pl.BlockSpec(a.shape, lambda i: (0,) * a.ndim)
    out = jax.ShapeDtypeStruct(gz.shape, F32)
    return pl.pallas_call(
        _gates_kernel,
        grid=(1,),
        in_specs=[full(gz), full(bif)],
        out_specs=[full(gz), full(gz)],
        out_shape=[out, out],
        name="gates",
    )(gz, bif)


def _mlstm_kernel(q_ref, k_ref, vt_ref, sot_ref, ab_ref, cm_ref, acol_ref, ng_ref, o_ref,
                  ct_ref, m_ref, *, ts, dh):
    L = MLSTM_CHUNK
    nh = M_HEADS
    j = pl.program_id(1)

    @pl.when(j == 0)
    def _():
        ct_ref[...] = jnp.zeros(ct_ref.shape, F32)
        m_ref[...] = jnp.zeros(m_ref.shape, F32)

    src = lax.broadcasted_iota(jnp.int32, (L, L), 0)
    tgt = lax.broadcasted_iota(jnp.int32, (L, L), 1)
    causal = src <= tgt
    ones_row = (lax.broadcasted_iota(jnp.int32, (LANES, L), 0) == 0).astype(BF16)

    m_state = [m_ref[hh:hh + 1, 0:1] for hh in range(nh)]
    for ci in range(ts // L):
        ts_ = slice(ci * L, (ci + 1) * L)
        for hh in range(nh):
            fs = slice(hh * dh, (hh + 1) * dh)
            a_row = ab_ref[hh:hh + 1, ts_]
            b_row = ab_ref[nh + hh:nh + hh + 1, ts_]
            m_prev = m_state[hh]
            c_row = jnp.maximum(cm_ref[hh:hh + 1, ts_], m_prev)
            a_col = acol_ref[ts_, hh:hh + 1]
            p_t = jnp.exp(jnp.where(causal, a_col - c_row, -jnp.inf))
            w_inter = jnp.exp(m_prev - c_row)

            qh = q_ref[ts_, fs]
            kh = k_ref[ts_, fs]
            vt_aug = jnp.concatenate([vt_ref[fs, ts_], ones_row], axis=0)
            ct = ct_ref[hh]
            both = lax.dot_general(jnp.concatenate([kh, ct.astype(BF16)], axis=0), qh, NT_DIMS,
                                   preferred_element_type=F32)
            sg_t = (both[:L] * p_t).astype(BF16)
            r_t = jnp.dot(vt_aug, sg_t, preferred_element_type=F32) + w_inter * both[L:]
            den = r_t[dh:dh + 1, :]
            m_t = b_row + c_row
            hv = r_t[:dh] * (1.0 / jnp.maximum(jnp.abs(den), jnp.exp(-m_t)))
            scale = lax.rsqrt(jnp.mean(hv * hv, axis=0, keepdims=True) + EPS)
            out_t = hv * scale * ng_ref[fs, :] * sot_ref[fs, ts_].astype(F32)
            o_ref[ts_, fs] = out_t.T.astype(BF16)

            c_last = c_row[:, L - 1:L]
            wk_row = jnp.exp(a_row - c_last)
            decay = jnp.exp(m_prev - c_last)
            vw = (vt_aug.astype(F32) * wk_row).astype(BF16)
            ct_ref[hh] = decay * ct + jnp.dot(vw, kh, preferred_element_type=F32)
            m_state[hh] = b_row[:, L - 1:L] + c_last
    for hh in range(nh):
        m_ref[hh:hh + 1, :] = jnp.broadcast_to(m_state[hh], (1, LANES))


def _mlstm(q, k, vt, sot, ab, cm, acol, ng_mat, *, batch, seq, ts):
    t, dm = q.shape
    dh = dm // M_HEADS
    ns = seq // ts
    tok = lambda width: pl.BlockSpec((ts, width), lambda b, j: (b * ns + j, 0))
    feat = lambda rows: pl.BlockSpec((rows, ts), lambda b, j: (0, b * ns + j))
    kern = functools.partial(_mlstm_kernel, ts=ts, dh=dh)
    return pl.pallas_call(
        kern,
        grid=(batch, ns),
        in_specs=[tok(dm), tok(dm), feat(dm), feat(dm), feat(ab.shape[0]), feat(cm.shape[0]),
                  tok(acol.shape[1]), _resident(ng_mat.shape)],
        out_specs=tok(dm),
        out_shape=jax.ShapeDtypeStruct((t, dm), BF16),
        scratch_shapes=[pltpu.VMEM((M_HEADS, dh + LANES, dh), F32),
                        pltpu.VMEM((SUBLANES, LANES), F32)],
        compiler_params=_params("arbitrary", "arbitrary"),
        name="mlstm",
    )(q, k, vt, sot, ab, cm, acol, ng_mat)


def _merge_kernel(x_ref, hm_ref, uc_ref, gate_ref, wm_ref, wc_ref, wo_ref, o_ref, *, d):
    bm = jnp.dot(hm_ref[...], wm_ref[...], preferred_element_type=F32)
    bc = jnp.dot(uc_ref[...], wc_ref[...], preferred_element_type=F32)
    mix = gate_ref[:, :d] * bm.astype(BF16) + gate_ref[:, d:] * bc.astype(BF16)
    o_ref[...] = x_ref[...] + jnp.dot(mix, wo_ref[...], preferred_element_type=F32)


def _merge(x2, hm, uc, gates, wm, wc, wo, *, tm):
    t, d = x2.shape
    tok = lambda width: pl.BlockSpec((tm, width), lambda i: (i, 0))
    return pl.pallas_call(
        functools.partial(_merge_kernel, d=d),
        grid=(t // tm,),
        in_specs=[tok(d), tok(d), tok(d), tok(2 * d)]
        + [_resident(a.shape) for a in (wm, wc, wo)],
        out_specs=tok(d),
        out_shape=jax.ShapeDtypeStruct((t, d), F32),
        compiler_params=_params("arbitrary"),
        name="merge",
    )(x2, hm, uc, gates, wm, wc, wo)


def _tail_kernel(x_ref, p_ref, gf_ref, wg_ref, wu_ref, wd_ref, gp_ref, wpg_ref, wpp_ref,
                 fg_ref, o_ref, *, ff_splits, final_norm):
    x1 = x_ref[...]
    f = _rms(x1, gf_ref[...]).astype(BF16)
    acc = x1
    for lo, hi in zip(ff_splits[:-1], ff_splits[1:]):
        gt = jnp.dot(f, wg_ref[:, lo:hi], preferred_element_type=F32)
        up = jnp.dot(f, wu_ref[:, lo:hi], preferred_element_type=F32)
        hid = _silu_bf16(gt) * up.astype(BF16)
        acc = acc + jnp.dot(hid, wd_ref[lo:hi, :], preferred_element_type=F32)
    x2 = acc
    gate = _sigmoid(jnp.dot(_rms(x2, gp_ref[...]).astype(BF16), wpg_ref[...],
                            preferred_element_type=F32))
    x3 = x2 + gate * jnp.dot(p_ref[...].astype(BF16), wpp_ref[...], preferred_element_type=F32)
    o_ref[...] = _rms(x3, fg_ref[...]) if final_norm else x3


def _tail(x1, p2, gf, wg, wu, wd, gp, wpg, wpp, fg, *, tm, final_norm):
    t, d = x1.shape
    dff = wg.shape[1]
    tiles = dff // MXU_TILE
    ff_splits = (0, -(-tiles // 2) * MXU_TILE, dff) if dff % MXU_TILE == 0 and tiles > 1 else (0, dff)
    kern = functools.partial(_tail_kernel, ff_splits=ff_splits, final_norm=final_norm)
    tok = lambda width: pl.BlockSpec((tm, width), lambda i: (i, 0))
    return pl.pallas_call(
        kern,
        grid=(t // tm,),
        in_specs=[tok(d), tok(p2.shape[1])]
        + [_resident(a.shape) for a in (gf, wg, wu, wd, gp, wpg, wpp, fg)],
        out_specs=tok(d),
        out_shape=jax.ShapeDtypeStruct((t, d), F32),
        compiler_params=_params("arbitrary"),
        name="tail",
    )(x1, p2, gf, wg, wu, wd, gp, wpg, wpp, fg)


def kernel(x, p, norm_mix_g, w_in, b_if, conv_qk_w, conv_qk_b, mh_norm_g, conf_conv_w, conf_conv_b, conf_ln_g, conf_ln_b, w_branch_m, w_branch_c, w_out, norm_ffn_g, w_ffn_gate, w_ffn_up, w_ffn_down, norm_ple_g, w_ple_gate, w_ple_proj, final_g):
    batch, seq, d = x.shape
    depth = w_in.shape[0]
    t = batch * seq
    nh = M_HEADS
    n_gate = 2 * nh
    tm = 512
    assert seq % tm == 0 and tm % (GATHER_ROWS * GATHER_STRIDE) == 0 and tm % MLSTM_CHUNK == 0
    assert d % (nh * LANES) == 0 and 2 * d == 4 * PROJ_CHUNK
    row = lambda a: a.reshape(1, -1).astype(F32)

    xf = x.reshape(t, d)
    for l in range(depth):
        w = w_in[l]
        wqk = w[:, :2 * d].astype(BF16)
        wvo_t = w[:, 2 * d:4 * d].astype(BF16).T
        wb = w[:, 4 * d + n_gate:].astype(BF16)
        wif_t = w[:, 4 * d:4 * d + n_gate].T.astype(BF16)

        q, k, vt, sot, uc, gates, gz = _inproj(
            xf, row(norm_mix_g[l]), wqk, wvo_t, wb, wif_t, conv_qk_w[l].astype(F32),
            row(conv_qk_b[l]), conf_conv_w[l].astype(F32), row(conf_conv_b[l]),
            row(conf_ln_g[l]), row(conf_ln_b[l]), seq=seq, tm=tm)

        ab, cm = _gates(gz, b_if[l].reshape(n_gate, 1).astype(F32))
        ng_mat = jnp.broadcast_to(mh_norm_g[l].astype(F32).reshape(d, 1), (d, MLSTM_CHUNK))
        hm = _mlstm(q, k, vt, sot, ab, cm, ab.T, ng_mat, batch=batch, seq=seq, ts=tm)

        x1 = _merge(xf, hm, uc, gates, w_branch_m[l].astype(BF16), w_branch_c[l].astype(BF16),
                    w_out[l].astype(BF16), tm=tm)

        xf = _tail(x1, p[l].reshape(t, -1), row(norm_ffn_g[l]), w_ffn_gate[l].astype(BF16),
                   w_ffn_up[l].astype(BF16), w_ffn_down[l].astype(BF16), row(norm_ple_g[l]),
                   w_ple_gate[l].astype(BF16), w_ple_proj[l].astype(BF16), row(final_g),
                   tm=tm, final_norm=(l == depth - 1))
    return xf.reshape(batch, seq, d)
```

```python
import functools

import jax
import jax.numpy as jnp
from jax import lax
from jax.experimental import pallas as pl
from jax.experimental.pallas import tpu as pltpu

EPS = 1e-6
M_HEADS = 4
MLSTM_CHUNK = 256
SUBLANES = 8
LANES = 128
MXU_TILE = 256
GATHER_STRIDE = 4
GATHER_ROWS = 16
QK_HALO = 8
CONF_HALO = 32
PROJ_CHUNK = 512
VMEM_LIMIT = 60 * 1024 * 1024

BF16 = jnp.bfloat16
F32 = jnp.float32
NT_DIMS = (((1,), (1,)), ((), ()))


def _sigmoid(z):
    return 0.5 * jnp.tanh(0.5 * z) + 0.5


def _silu(z):
    h = 0.5 * z
    return h * jnp.tanh(h) + h


def _sigmoid_bf16(z):
    return _sigmoid(z.astype(BF16))


def _silu_bf16(z):
    return _silu(z.astype(BF16))


def _rms(xf, g):
    return xf * lax.rsqrt(jnp.mean(xf * xf, axis=-1, keepdims=True) + EPS) * g


def _resident(shape):
    nd = len(shape)
    return pl.BlockSpec(shape, lambda *_: (0,) * nd, pipeline_mode=pl.Buffered(1))


def _params(*semantics):
    return pltpu.CompilerParams(dimension_semantics=semantics, vmem_limit_bytes=VMEM_LIMIT)


def _fill_slabs(buf, val, first, *, slab0, halo, rows):
    for j in range(val.shape[1] // LANES):
        s = slab0 + j
        prev = buf[s, pl.ds(rows, halo), :]
        buf[s, pl.ds(0, halo), :] = jnp.where(first, 0.0, prev)
        buf[s, pl.ds(halo, rows), :] = val[:, j * LANES:(j + 1) * LANES]


def _causal_conv(src, dst, w_ref, b_ref, *, slabs, rows, halo):
    taps = w_ref.shape[0]
    st = GATHER_STRIDE
    span = GATHER_ROWS * st
    for s in slabs:
        cs = slice(s * LANES, (s + 1) * LANES)
        w = [jnp.broadcast_to(w_ref[k:k + 1, cs], (GATHER_ROWS, LANES)) for k in range(taps)]
        bias = jnp.broadcast_to(b_ref[0:1, cs], (GATHER_ROWS, LANES))
        for g in range(rows // span):
            acc = [bias] * st
            for m in range(st + taps - 1):
                uv = src[s, pl.ds(halo + g * span + m - (taps - 1), GATHER_ROWS, stride=st), :]
                for a in range(st):
                    if 0 <= m - a < taps:
                        acc[a] = acc[a] + w[m - a] * uv
            for a in range(st):
                dst[s, pl.ds(g * span + a, GATHER_ROWS, stride=st), :] = acc[a]


def _from_slabs(buf, lo, hi):
    return jnp.concatenate([buf[s] for s in range(lo, hi)], axis=-1)


def _inproj_kernel(x_ref, g_ref, wqk_ref, wvo_ref, wb_ref, wif_ref, qw_ref, qb_ref, cw_ref,
                   cb_ref, lg_ref, lb_ref,
                   q_ref, k_ref, vt_ref, sot_ref, uc_ref, gate_ref, if_ref,
                   qbuf, ubuf, ybuf_q, ybuf_c, *, tm, d, tiles_per_seq, q_scale):
    i = pl.program_id(0)
    nsl = d // LANES
    pc = PROJ_CHUNK
    spc = pc // LANES

    @pl.when(i == 0)
    def _():
        qbuf[:, pl.ds(tm, QK_HALO), :] = jnp.zeros((2 * nsl, QK_HALO, LANES), F32)
        ubuf[:, pl.ds(tm, CONF_HALO), :] = jnp.zeros((nsl, CONF_HALO, LANES), F32)

    first = i % tiles_per_seq == 0
    h = _rms(x_ref[...], g_ref[...]).astype(BF16)

    def proj(wt_ref, col0, n):
        return lax.dot_general(h, wt_ref[col0:col0 + n, :], NT_DIMS, preferred_element_type=F32)

    def glu(c):
        hw = pc // 2
        a = proj(wb_ref, c * hw, hw)
        ga = proj(wb_ref, d + c * hw, hw)
        _fill_slabs(ubuf, a * _sigmoid(ga), first, slab0=c * (spc // 2), halo=CONF_HALO, rows=tm)

    def conf_conv(c):
        _causal_conv(ubuf, ybuf_c, cw_ref, cb_ref, rows=tm, halo=CONF_HALO,
                     slabs=range(c * (spc // 2), (c + 1) * (spc // 2)))

    def qk(c):
        _fill_slabs(qbuf, proj(wqk_ref, c * pc, pc), first, slab0=c * spc, halo=QK_HALO, rows=tm)

    def qk_conv(c):
        _causal_conv(qbuf, ybuf_q, qw_ref, qb_ref, rows=tm, halo=QK_HALO,
                     slabs=range(c * spc, (c + 1) * spc))

    def vo(c):
        r = lax.dot_general(wvo_ref[c * pc:(c + 1) * pc, :], h, NT_DIMS, preferred_element_type=F32)
        if (c + 1) * pc <= d:
            vt_ref[c * pc:(c + 1) * pc, :] = r.astype(BF16)
        else:
            sot_ref[c * pc - d:(c + 1) * pc - d, :] = _sigmoid_bf16(r)

    def gate(c):
        gate_ref[:, c * pc:(c + 1) * pc] = _sigmoid_bf16(proj(wb_ref, 2 * d + c * pc, pc))

    def conf_norm():
        y = _from_slabs(ybuf_c, 0, nsl)
        yc = y - jnp.mean(y, axis=-1, keepdims=True)
        var = jnp.mean(yc * yc, axis=-1, keepdims=True)
        uc_ref[...] = _silu_bf16(yc * lax.rsqrt(var + EPS) * lg_ref[...] + lb_ref[...])

    assert 2 * d // pc == 4
    glu(0)
    glu(1); conf_conv(0)
    glu(2); qk(0); conf_conv(1)
    glu(3); qk(1); conf_conv(2)
    qk(2); qk(3); conf_conv(3)
    vo(0); vo(1); qk_conv(0)
    conf_norm(); vo(2); qk_conv(1)
    vo(3); gate(0); qk_conv(2)
    gate(1); qk_conv(3)
    q_ref[...] = _silu_bf16(_from_slabs(ybuf_q, 0, nsl)) * q_scale
    gate(2)
    k_ref[...] = _silu_bf16(_from_slabs(ybuf_q, nsl, 2 * nsl))
    gate(3)
    if_ref[...] = lax.dot_general(wif_ref[...], h, NT_DIMS, preferred_element_type=F32)


def _inproj(x2, g, wqk, wvo_t, wb, wif_t, qw, qb, cw, cb, lg, lb, *, seq, tm):
    t, d = x2.shape
    nsl = d // LANES
    kern = functools.partial(_inproj_kernel, tm=tm, d=d, tiles_per_seq=seq // tm,
                             q_scale=float((d // M_HEADS) ** -0.5))
    tok = lambda width: pl.BlockSpec((tm, width), lambda i: (i, 0))
    feat = lambda rows: pl.BlockSpec((rows, tm), lambda i: (0, i))
    n_gate = wif_t.shape[0]
    out_shape = [jax.ShapeDtypeStruct((t, d), BF16), jax.ShapeDtypeStruct((t, d), BF16),
                 jax.ShapeDtypeStruct((d, t), BF16), jax.ShapeDtypeStruct((d, t), BF16),
                 jax.ShapeDtypeStruct((t, d), BF16), jax.ShapeDtypeStruct((t, 2 * d), BF16),
                 jax.ShapeDtypeStruct((n_gate, t), F32)]
    consts = (g, wqk, wvo_t, wb, wif_t, qw, qb, cw, cb, lg, lb)
    return pl.pallas_call(
        kern,
        grid=(t // tm,),
        in_specs=[tok(d)] + [_resident(a.shape) for a in consts],
        out_specs=[tok(d), tok(d), feat(d), feat(d), tok(d), tok(2 * d), feat(n_gate)],
        out_shape=out_shape,
        scratch_shapes=[pltpu.VMEM((2 * nsl, tm + QK_HALO, LANES), F32),
                        pltpu.VMEM((nsl, tm + CONF_HALO, LANES), F32),
                        pltpu.VMEM((2 * nsl, tm, LANES), F32),
                        pltpu.VMEM((nsl, tm, LANES), F32)],
        compiler_params=_params("arbitrary"),
        name="inproj",
    )(x2, *consts)


def _chunk_scan(x, op, identity):
    lane_in_chunk = lax.broadcasted_iota(jnp.int32, x.shape, 1) % MLSTM_CHUNK
    sh = 1
    while sh < MLSTM_CHUNK:
        x = op(x, jnp.where(lane_in_chunk >= sh, pltpu.roll(x, sh, axis=1), identity))
        sh *= 2
    return x


def _gates_kernel(z_ref, bif_ref, ab_ref, cm_ref):
    nh = M_HEADS
    pre = z_ref[...] + bif_ref[...]
    lf = jnp.minimum(pre, 0.0) - jnp.log(1.0 + jnp.exp(-jnp.abs(pre)))
    b = _chunk_scan(lf, jnp.add, 0.0)
    a = pre - pltpu.roll(b, nh, axis=0)
    is_i = lax.broadcasted_iota(jnp.int32, pre.shape, 0) < nh
    ab_ref[...] = jnp.where(is_i, a, b)
    cm_ref[...] = _chunk_scan(a, jnp.maximum, -jnp.inf)


def _gates(gz, bif):
    full = lambda a: pl.BlockSpec(a.shape, lambda i: (0,) * a.ndim)
    out = jax.ShapeDtypeStruct(gz.shape, F32)
    return pl.pallas_call(
        _gates_kernel,
        grid=(1,),
        in_specs=[full(gz), full(bif)],
        out_specs=[full(gz), full(gz)],
        out_shape=[out, out],
        name="gates",
    )(gz, bif)


def _mlstm_kernel(q_ref, k_ref, vt_ref, sot_ref, ab_ref, cm_ref, acol_ref, ng_ref, o_ref,
                  ct_ref, m_ref, *, ts, dh):
    L = MLSTM_CHUNK
    nh = M_HEADS
    j = pl.program_id(1)

    @pl.when(j == 0)
    def _():
        ct_ref[...] = jnp.zeros(ct_ref.shape, F32)
        m_ref[...] = jnp.zeros(m_ref.shape, F32)

    src = lax.broadcasted_iota(jnp.int32, (L, L), 0)
    tgt = lax.broadcasted_iota(jnp.int32, (L, L), 1)
    causal = src <= tgt
    ones_row = (lax.broadcasted_iota(jnp.int32, (LANES, L), 0) == 0).astype(BF16)

    m_state = [m_ref[hh:hh + 1, 0:1] for hh in range(nh)]
    for ci in range(ts // L):
        ts_ = slice(ci * L, (ci + 1) * L)
        for hh in range(nh):
            fs = slice(hh * dh, (hh + 1) * dh)
            a_row = ab_ref[hh:hh + 1, ts_]
            b_row = ab_ref[nh + hh:nh + hh + 1, ts_]
            m_prev = m_state[hh]
            c_row = jnp.maximum(cm_ref[hh:hh + 1, ts_], m_prev)
            a_col = acol_ref[ts_, hh:hh + 1]
            p_t = jnp.exp(jnp.where(causal, a_col - c_row, -jnp.inf))
            w_inter = jnp.exp(m_prev - c_row)

            qh = q_ref[ts_, fs]
            kh = k_ref[ts_, fs]
            vt_aug = jnp.concatenate([vt_ref[fs, ts_], ones_row], axis=0)
            ct = ct_ref[hh]
            both = lax.dot_general(jnp.concatenate([kh, ct.astype(BF16)], axis=0), qh, NT_DIMS,
                                   preferred_element_type=F32)
            sg_t = (both[:L] * p_t).astype(BF16)
            r_t = jnp.dot(vt_aug, sg_t, preferred_element_type=F32) + w_inter * both[L:]
            den = r_t[dh:dh + 1, :]
            m_t = b_row + c_row
            hv = r_t[:dh] * (1.0 / jnp.maximum(jnp.abs(den), jnp.exp(-m_t)))
            scale = lax.rsqrt(jnp.mean(hv * hv, axis=0, keepdims=True) + EPS)
            out_t = hv * scale * ng_ref[fs, :] * sot_ref[fs, ts_].astype(F32)
            o_ref[ts_, fs] = out_t.T.astype(BF16)

            c_last = c_row[:, L - 1:L]
            wk_row = jnp.exp(a_row - c_last)
            decay = jnp.exp(m_prev - c_last)
            vw = vt_aug * wk_row.astype(BF16)
            ct_ref[hh] = decay * ct + jnp.dot(vw, kh, preferred_element_type=F32)
            m_state[hh] = b_row[:, L - 1:L] + c_last
    for hh in range(nh):
        m_ref[hh:hh + 1, :] = jnp.broadcast_to(m_state[hh], (1, LANES))


def _mlstm(q, k, vt, sot, ab, cm, acol, ng_mat, *, batch, seq, ts):
    t, dm = q.shape
    dh = dm // M_HEADS
    ns = seq // ts
    tok = lambda width: pl.BlockSpec((ts, width), lambda b, j: (b * ns + j, 0))
    feat = lambda rows: pl.BlockSpec((rows, ts), lambda b, j: (0, b * ns + j))
    kern = functools.partial(_mlstm_kernel, ts=ts, dh=dh)
    return pl.pallas_call(
        kern,
        grid=(batch, ns),
        in_specs=[tok(dm), tok(dm), feat(dm), feat(dm), feat(ab.shape[0]), feat(cm.shape[0]),
                  tok(acol.shape[1]), _resident(ng_mat.shape)],
        out_specs=tok(dm),
        out_shape=jax.ShapeDtypeStruct((t, dm), BF16),
        scratch_shapes=[pltpu.VMEM((M_HEADS, dh + LANES, dh), F32),
                        pltpu.VMEM((SUBLANES, LANES), F32)],
        compiler_params=_params("arbitrary", "arbitrary"),
        name="mlstm",
    )(q, k, vt, sot, ab, cm, acol, ng_mat)


def _merge_kernel(x_ref, hm_ref, uc_ref, gate_ref, wm_ref, wc_ref, wo_ref, o_ref, *, d):
    bm = jnp.dot(hm_ref[...], wm_ref[...], preferred_element_type=F32)
    bc = jnp.dot(uc_ref[...], wc_ref[...], preferred_element_type=F32)
    mix = gate_ref[:, :d] * bm.astype(BF16) + gate_ref[:, d:] * bc.astype(BF16)
    o_ref[...] = x_ref[...] + jnp.dot(mix, wo_ref[...], preferred_element_type=F32)


def _merge(x2, hm, uc, gates, wm, wc, wo, *, tm):
    t, d = x2.shape
    tok = lambda width: pl.BlockSpec((tm, width), lambda i: (i, 0))
    return pl.pallas_call(
        functools.partial(_merge_kernel, d=d),
        grid=(t // tm,),
        in_specs=[tok(d), tok(d), tok(d), tok(2 * d)]
        + [_resident(a.shape) for a in (wm, wc, wo)],
        out_specs=tok(d),
        out_shape=jax.ShapeDtypeStruct((t, d), F32),
        compiler_params=_params("arbitrary"),
        name="merge",
    )(x2, hm, uc, gates, wm, wc, wo)


def _tail_kernel(x_ref, p_ref, gf_ref, wg_ref, wu_ref, wd_ref, gp_ref, wpg_ref, wpp_ref,
                 fg_ref, o_ref, *, ff_splits, final_norm):
    x1 = x_ref[...]
    f = _rms(x1, gf_ref[...]).astype(BF16)
    acc = x1
    for lo, hi in zip(ff_splits[:-1], ff_splits[1:]):
        gt = jnp.dot(f, wg_ref[:, lo:hi], preferred_element_type=F32)
        up = jnp.dot(f, wu_ref[:, lo:hi], preferred_element_type=F32)
        hid = _silu_bf16(gt) * up.astype(BF16)
        acc = acc + jnp.dot(hid, wd_ref[lo:hi, :], preferred_element_type=F32)
    x2 = acc
    gate = _sigmoid(jnp.dot(_rms(x2, gp_ref[...]).astype(BF16), wpg_ref[...],
                            preferred_element_type=F32))
    x3 = x2 + gate * jnp.dot(p_ref[...].astype(BF16), wpp_ref[...], preferred_element_type=F32)
    o_ref[...] = _rms(x3, fg_ref[...]) if final_norm else x3


def _tail(x1, p2, gf, wg, wu, wd, gp, wpg, wpp, fg, *, tm, final_norm):
    t, d = x1.shape
    dff = wg.shape[1]
    tiles = dff // MXU_TILE
    ff_splits = (0, -(-tiles // 2) * MXU_TILE, dff) if dff % MXU_TILE == 0 and tiles > 1 else (0, dff)
    kern = functools.partial(_tail_kernel, ff_splits=ff_splits, final_norm=final_norm)
    tok = lambda width: pl.BlockSpec((tm, width), lambda i: (i, 0))
    return pl.pallas_call(
        kern,
        grid=(t // tm,),
        in_specs=[tok(d), tok(p2.shape[1])]
        + [_resident(a.shape) for a in (gf, wg, wu, wd, gp, wpg, wpp, fg)],
        out_specs=tok(d),
        out_shape=jax.ShapeDtypeStruct((t, d), F32),
        compiler_params=_params("arbitrary"),
        name="tail",
    )(x1, p2, gf, wg, wu, wd, gp, wpg, wpp, fg)


def kernel(x, p, norm_mix_g, w_in, b_if, conv_qk_w, conv_qk_b, mh_norm_g, conf_conv_w, conf_conv_b, conf_ln_g, conf_ln_b, w_branch_m, w_branch_c, w_out, norm_ffn_g, w_ffn_gate, w_ffn_up, w_ffn_down, norm_ple_g, w_ple_gate, w_ple_proj, final_g):
    batch, seq, d = x.shape
    depth = w_in.shape[0]
    t = batch * seq
    nh = M_HEADS
    n_gate = 2 * nh
    tm = 512
    assert seq % (2 * tm) == 0 and tm % (GATHER_ROWS * GATHER_STRIDE) == 0 and tm % MLSTM_CHUNK == 0
    assert d % (nh * LANES) == 0 and 2 * d == 4 * PROJ_CHUNK
    row = lambda a: a.reshape(1, -1).astype(F32)

    xf = x.reshape(t, d)
    for l in range(depth):
        wt = w_in[l].astype(BF16).T
        wqk = wt[:2 * d]
        wvo_t = wt[2 * d:4 * d]
        wb = wt[4 * d + n_gate:]
        wif_t = wt[4 * d:4 * d + n_gate]

        q, k, vt, sot, uc, gates, gz = _inproj(
            xf, row(norm_mix_g[l]), wqk, wvo_t, wb, wif_t, conv_qk_w[l].astype(F32),
            row(conv_qk_b[l]), conf_conv_w[l].astype(F32), row(conf_conv_b[l]),
            row(conf_ln_g[l]), row(conf_ln_b[l]), seq=seq, tm=tm)

        ab, cm = _gates(gz, b_if[l].reshape(n_gate, 1).astype(F32))
        ng_mat = jnp.broadcast_to(mh_norm_g[l].astype(F32).reshape(d, 1), (d, MLSTM_CHUNK))
        hm = _mlstm(q, k, vt, sot, ab, cm, ab.T, ng_mat, batch=batch, seq=seq, ts=2 * tm)

        x1 = _merge(xf, hm, uc, gates, w_branch_m[l].astype(BF16), w_branch_c[l].astype(BF16),
                    w_out[l].astype(BF16), tm=tm)

        xf = _tail(x1, p[l].reshape(t, -1), row(norm_ffn_g[l]), w_ffn_gate[l].astype(BF16),
                   w_ffn_up[l].astype(BF16), w_ffn_down[l].astype(BF16), row(norm_ple_g[l]),
                   w_ple_gate[l].astype(BF16), w_ple_proj[l].astype(BF16), row(final_g),
                   tm=tm, final_norm=(l == depth - 1))
    return xf.reshape(batch, seq, d)
```

```python
import functools

import jax
import jax.numpy as jnp
from jax import lax
from jax.experimental import pallas as pl
from jax.experimental.pallas import tpu as pltpu

EPS = 1e-6
M_HEADS = 4
MLSTM_CHUNK = 256
SUBLANES = 8
LANES = 128
MXU_TILE = 256
GATHER_STRIDE = 4
GATHER_ROWS = 16
QK_HALO = 8
CONF_HALO = 32
PROJ_CHUNK = 512
VMEM_LIMIT = 60 * 1024 * 1024

BF16 = jnp.bfloat16
F32 = jnp.float32
NT_DIMS = (((1,), (1,)), ((), ()))


def _sigmoid(z):
    return 0.5 * jnp.tanh(0.5 * z) + 0.5


def _silu(z):
    h = 0.5 * z
    return h * jnp.tanh(h) + h


def _sigmoid_bf16(z):
    return _sigmoid(z.astype(BF16))


def _silu_bf16(z):
    return _silu(z.astype(BF16))


def _rms(xf):
    return xf * lax.rsqrt(jnp.mean(xf * xf, axis=-1, keepdims=True) + EPS)


def _scaled_rows(w, g):
    return (w.astype(F32) * g.astype(F32)[:, None]).astype(BF16)


def _resident(shape):
    nd = len(shape)
    return pl.BlockSpec(shape, lambda *_: (0,) * nd, pipeline_mode=pl.Buffered(1))


def _params(*semantics):
    return pltpu.CompilerParams(dimension_semantics=semantics, vmem_limit_bytes=VMEM_LIMIT)


def _fill_slabs(buf, val, first, *, slab0, halo, rows):
    for j in range(val.shape[1] // LANES):
        s = slab0 + j
        prev = buf[s, pl.ds(rows, halo), :]
        buf[s, pl.ds(0, halo), :] = jnp.where(first, 0.0, prev)
        buf[s, pl.ds(halo, rows), :] = val[:, j * LANES:(j + 1) * LANES]


def _causal_conv(src, dst, w_ref, b_ref, *, slabs, rows, halo):
    taps = w_ref.shape[0]
    st = GATHER_STRIDE
    span = GATHER_ROWS * st
    for s in slabs:
        cs = slice(s * LANES, (s + 1) * LANES)
        w = [jnp.broadcast_to(w_ref[k:k + 1, cs], (GATHER_ROWS, LANES)) for k in range(taps)]
        bias = jnp.broadcast_to(b_ref[0:1, cs], (GATHER_ROWS, LANES))
        for g in range(rows // span):
            acc = [bias] * st
            for m in range(st + taps - 1):
                uv = src[s, pl.ds(halo + g * span + m - (taps - 1), GATHER_ROWS, stride=st), :]
                for a in range(st):
                    if 0 <= m - a < taps:
                        acc[a] = acc[a] + w[m - a] * uv
            for a in range(st):
                dst[s, pl.ds(g * span + a, GATHER_ROWS, stride=st), :] = acc[a]


def _from_slabs(buf, lo, hi):
    return jnp.concatenate([buf[s] for s in range(lo, hi)], axis=-1)


def _inproj_kernel(x_ref, wqk_ref, wvo_ref, wb_ref, wif_ref, qw_ref, qb_ref, cw_ref,
                   cb_ref, lg_ref, lb_ref,
                   qk_ref, vs_ref, uc_ref, gate_ref, if_ref,
                   qbuf, ubuf, ybuf_q, ybuf_c, *, tm, d, tiles_per_seq, q_scale):
    i = pl.program_id(0)
    nsl = d // LANES
    pc = PROJ_CHUNK
    spc = pc // LANES

    @pl.when(i == 0)
    def _():
        qbuf[:, pl.ds(tm, QK_HALO), :] = jnp.zeros((2 * nsl, QK_HALO, LANES), F32)
        ubuf[:, pl.ds(tm, CONF_HALO), :] = jnp.zeros((nsl, CONF_HALO, LANES), F32)

    first = i % tiles_per_seq == 0
    h = _rms(x_ref[...]).astype(BF16)

    def proj(wt_ref, col0, n):
        return lax.dot_general(h, wt_ref[col0:col0 + n, :], NT_DIMS, preferred_element_type=F32)

    def glu(c):
        hw = pc // 2
        a = proj(wb_ref, c * hw, hw)
        ga = proj(wb_ref, d + c * hw, hw)
        _fill_slabs(ubuf, a * _sigmoid(ga), first, slab0=c * (spc // 2), halo=CONF_HALO, rows=tm)

    def conf_conv(c):
        _causal_conv(ubuf, ybuf_c, cw_ref, cb_ref, rows=tm, halo=CONF_HALO,
                     slabs=range(c * (spc // 2), (c + 1) * (spc // 2)))

    def qk(c):
        _fill_slabs(qbuf, proj(wqk_ref, c * pc, pc), first, slab0=c * spc, halo=QK_HALO, rows=tm)

    def qk_conv(c):
        _causal_conv(qbuf, ybuf_q, qw_ref, qb_ref, rows=tm, halo=QK_HALO,
                     slabs=range(c * spc, (c + 1) * spc))

    def vo(c):
        r = lax.dot_general(wvo_ref[c * pc:(c + 1) * pc, :], h, NT_DIMS, preferred_element_type=F32)
        vs_ref[c * pc:(c + 1) * pc, :] = r.astype(BF16) if (c + 1) * pc <= d else _sigmoid_bf16(r)

    def gate(c):
        gate_ref[:, c * pc:(c + 1) * pc] = _sigmoid_bf16(proj(wb_ref, 2 * d + c * pc, pc))

    def conf_norm():
        y = _from_slabs(ybuf_c, 0, nsl)
        yc = y - jnp.mean(y, axis=-1, keepdims=True)
        var = jnp.mean(yc * yc, axis=-1, keepdims=True)
        uc_ref[...] = _silu_bf16(yc * lax.rsqrt(var + EPS) * lg_ref[...] + lb_ref[...])

    assert 2 * d // pc == 4
    glu(0)
    glu(1); conf_conv(0)
    glu(2); qk(0); conf_conv(1)
    glu(3); qk(1); conf_conv(2)
    qk(2); qk(3); conf_conv(3)
    vo(0); vo(1); qk_conv(0)
    conf_norm(); vo(2); qk_conv(1)
    vo(3); gate(0); qk_conv(2)
    gate(1); qk_conv(3)
    qk_ref[:, :d] = _silu_bf16(_from_slabs(ybuf_q, 0, nsl)) * q_scale
    gate(2)
    qk_ref[:, d:] = _silu_bf16(_from_slabs(ybuf_q, nsl, 2 * nsl))
    gate(3)
    if_ref[...] = lax.dot_general(wif_ref[...], h, NT_DIMS, preferred_element_type=F32)


def _inproj(x2, wqk, wvo_t, wb, wif_t, qw, qb, cw, cb, lg, lb, *, seq, tm):
    t, d = x2.shape
    nsl = d // LANES
    kern = functools.partial(_inproj_kernel, tm=tm, d=d, tiles_per_seq=seq // tm,
                             q_scale=float((d // M_HEADS) ** -0.5))
    tok = lambda width: pl.BlockSpec((tm, width), lambda i: (i, 0))
    feat = lambda rows: pl.BlockSpec((rows, tm), lambda i: (0, i))
    n_gate = wif_t.shape[0]
    out_shape = [jax.ShapeDtypeStruct((t, 2 * d), BF16),
                 jax.ShapeDtypeStruct((2 * d, t), BF16),
                 jax.ShapeDtypeStruct((t, d), BF16),
                 jax.ShapeDtypeStruct((t, 2 * d), BF16),
                 jax.ShapeDtypeStruct((n_gate, t), F32)]
    consts = (wqk, wvo_t, wb, wif_t, qw, qb, cw, cb, lg, lb)
    return pl.pallas_call(
        kern,
        grid=(t // tm,),
        in_specs=[tok(d)] + [_resident(a.shape) for a in consts],
        out_specs=[tok(2 * d), feat(2 * d), tok(d), tok(2 * d), feat(n_gate)],
        out_shape=out_shape,
        scratch_shapes=[pltpu.VMEM((2 * nsl, tm + QK_HALO, LANES), F32),
                        pltpu.VMEM((nsl, tm + CONF_HALO, LANES), F32),
                        pltpu.VMEM((2 * nsl, tm, LANES), F32),
                        pltpu.VMEM((nsl, tm, LANES), F32)],
        compiler_params=_params("arbitrary"),
        name="inproj",
    )(x2, *consts)


def _chunk_scan(x, op, identity):
    lane_in_chunk = lax.broadcasted_iota(jnp.int32, x.shape, 1) % MLSTM_CHUNK
    sh = 1
    while sh < MLSTM_CHUNK:
        x = op(x, jnp.where(lane_in_chunk >= sh, pltpu.roll(x, sh, axis=1), identity))
        sh *= 2
    return x


def _gates_kernel(z_ref, bif_ref, ab_ref, cm_ref):
    nh = M_HEADS
    pre = z_ref[...] + bif_ref[...]
    lf = jnp.minimum(pre, 0.0) - jnp.log(1.0 + jnp.exp(-jnp.abs(pre)))
    b = _chunk_scan(lf, jnp.add, 0.0)
    a = pre - pltpu.roll(b, nh, axis=0)
    is_i = lax.broadcasted_iota(jnp.int32, pre.shape, 0) < nh
    ab_ref[...] = jnp.where(is_i, a, b)
    cm_ref[...] = _chunk_scan(a, jnp.maximum, -jnp.inf)


def _gates(gz, bif):
    full = lambda a: pl.BlockSpec(a.shape, lambda i: (0,) * a.ndim)
    out = jax.ShapeDtypeStruct(gz.shape, F32)
    return pl.pallas_call(
        _gates_kernel,
        grid=(1,),
        in_specs=[full(gz), full(bif)],
        out_specs=[full(gz), full(gz)],
        out_shape=[out, out],
        name="gates",
    )(gz, bif)


def _mlstm_kernel(qk_ref, vs_ref, ab_ref, cm_ref, acol_ref, o_ref, ct_ref, m_ref, *, ts, dh):
    L = MLSTM_CHUNK
    nh = M_HEADS
    dm = nh * dh
    j = pl.program_id(1)

    @pl.when(j == 0)
    def _():
        ct_ref[...] = jnp.zeros(ct_ref.shape, F32)
        m_ref[...] = jnp.zeros(m_ref.shape, F32)

    src = lax.broadcasted_iota(jnp.int32, (L, L), 0)
    tgt = lax.broadcasted_iota(jnp.int32, (L, L), 1)
    causal = src <= tgt
    ones_row = (lax.broadcasted_iota(jnp.int32, (LANES, L), 0) == 0).astype(BF16)

    m_state = [m_ref[hh:hh + 1, 0:1] for hh in range(nh)]
    for ci in range(ts // L):
        ts_ = slice(ci * L, (ci + 1) * L)
        for hh in range(nh):
            fs = slice(hh * dh, (hh + 1) * dh)
            a_row = ab_ref[hh:hh + 1, ts_]
            b_row = ab_ref[nh + hh:nh + hh + 1, ts_]
            m_prev = m_state[hh]
            c_row = jnp.maximum(cm_ref[hh:hh + 1, ts_], m_prev)
            a_col = acol_ref[ts_, hh:hh + 1]
            p_t = jnp.exp(jnp.where(causal, a_col - c_row, -jnp.inf))
            w_inter = jnp.exp(m_prev - c_row)

            qh = qk_ref[ts_, fs]
            kh = qk_ref[ts_, dm + hh * dh:dm + (hh + 1) * dh]
            vt_aug = jnp.concatenate([vs_ref[fs, ts_], ones_row], axis=0)
            ct = ct_ref[hh]
            both = lax.dot_general(jnp.concatenate([kh, ct.astype(BF16)], axis=0), qh, NT_DIMS,
                                   preferred_element_type=F32)
            sg_t = (both[:L] * p_t).astype(BF16)
            r_t = jnp.dot(vt_aug, sg_t, preferred_element_type=F32) + w_inter * both[L:]
            den = r_t[dh:dh + 1, :]
            m_t = b_row + c_row
            hv = r_t[:dh] * (1.0 / jnp.maximum(jnp.abs(den), jnp.exp(-m_t)))
            scale = lax.rsqrt(jnp.mean(hv * hv, axis=0, keepdims=True) + EPS)
            out_t = hv * scale * vs_ref[dm + hh * dh:dm + (hh + 1) * dh, ts_].astype(F32)
            o_ref[ts_, fs] = out_t.T.astype(BF16)

            c_last = c_row[:, L - 1:L]
            wk_row = jnp.exp(a_row - c_last)
            decay = jnp.exp(m_prev - c_last)
            vw = vt_aug * wk_row.astype(BF16)
            ct_ref[hh] = decay * ct + jnp.dot(vw, kh, preferred_element_type=F32)
            m_state[hh] = b_row[:, L - 1:L] + c_last
    for hh in range(nh):
        m_ref[hh:hh + 1, :] = jnp.broadcast_to(m_state[hh], (1, LANES))


def _mlstm(qk, vs, ab, cm, acol, *, batch, seq, ts):
    t = qk.shape[0]
    dm = qk.shape[1] // 2
    dh = dm // M_HEADS
    ns = seq // ts
    tok = lambda width: pl.BlockSpec((ts, width), lambda b, j: (b * ns + j, 0))
    feat = lambda rows: pl.BlockSpec((rows, ts), lambda b, j: (0, b * ns + j))
    kern = functools.partial(_mlstm_kernel, ts=ts, dh=dh)
    return pl.pallas_call(
        kern,
        grid=(batch, ns),
        in_specs=[tok(2 * dm), feat(2 * dm), feat(ab.shape[0]), feat(cm.shape[0]),
                  tok(acol.shape[1])],
        out_specs=tok(dm),
        out_shape=jax.ShapeDtypeStruct((t, dm), BF16),
        scratch_shapes=[pltpu.VMEM((M_HEADS, dh + LANES, dh), F32),
                        pltpu.VMEM((SUBLANES, LANES), F32)],
        compiler_params=_params("arbitrary", "arbitrary"),
        name="mlstm",
    )(qk, vs, ab, cm, acol)


def _tail_kernel(x_ref, hm_ref, uc_ref, gate_ref, p_ref, wm_ref, wc_ref, wo_ref,
                 wg_ref, wu_ref, wd_ref, wpg_ref, wpp_ref, fg_ref, o_ref,
                 *, d, ff_splits, final_norm):
    bm = jnp.dot(hm_ref[...], wm_ref[...], preferred_element_type=F32)
    bc = jnp.dot(uc_ref[...], wc_ref[...], preferred_element_type=F32)
    mix = gate_ref[:, :d] * bm.astype(BF16) + gate_ref[:, d:] * bc.astype(BF16)
    x1 = x_ref[...] + jnp.dot(mix, wo_ref[...], preferred_element_type=F32)

    f = _rms(x1).astype(BF16)
    acc = x1
    for lo, hi in zip(ff_splits[:-1], ff_splits[1:]):
        gt = jnp.dot(f, wg_ref[:, lo:hi], preferred_element_type=F32)
        up = jnp.dot(f, wu_ref[:, lo:hi], preferred_element_type=F32)
        hid = _silu_bf16(gt) * up.astype(BF16)
        acc = acc + jnp.dot(hid, wd_ref[lo:hi, :], preferred_element_type=F32)
    x2 = acc
    gate = _sigmoid(jnp.dot(_rms(x2).astype(BF16), wpg_ref[...], preferred_element_type=F32))
    x3 = x2 + gate * jnp.dot(p_ref[...].astype(BF16), wpp_ref[...], preferred_element_type=F32)
    o_ref[...] = _rms(x3) * fg_ref[...] if final_norm else x3


def _tail(x2, hm, uc, gates, p2, wm, wc, wo, wg, wu, wd, wpg, wpp, fg, *, tm, final_norm):
    t, d = x2.shape
    dff = wg.shape[1]
    tiles = dff // MXU_TILE
    ff_splits = (0, -(-tiles // 2) * MXU_TILE, dff) if dff % MXU_TILE == 0 and tiles > 1 else (0, dff)
    kern = functools.partial(_tail_kernel, d=d, ff_splits=ff_splits, final_norm=final_norm)
    tok = lambda width: pl.BlockSpec((tm, width), lambda i: (i, 0))
    consts = (wm, wc, wo, wg, wu, wd, wpg, wpp, fg)
    return pl.pallas_call(
        kern,
        grid=(t // tm,),
        in_specs=[tok(d), tok(d), tok(d), tok(2 * d), tok(p2.shape[1])]
        + [_resident(a.shape) for a in consts],
        out_specs=tok(d),
        out_shape=jax.ShapeDtypeStruct((t, d), F32),
        compiler_params=_params("arbitrary"),
        name="tail",
    )(x2, hm, uc, gates, p2, *consts)


def kernel(x, p, norm_mix_g, w_in, b_if, conv_qk_w, conv_qk_b, mh_norm_g, conf_conv_w, conf_conv_b, conf_ln_g, conf_ln_b, w_branch_m, w_branch_c, w_out, norm_ffn_g, w_ffn_gate, w_ffn_up, w_ffn_down, norm_ple_g, w_ple_gate, w_ple_proj, final_g):
    batch, seq, d = x.shape
    depth = w_in.shape[0]
    t = batch * seq
    nh = M_HEADS
    n_gate = 2 * nh
    tm = 512
    assert seq % (2 * tm) == 0 and tm % (GATHER_ROWS * GATHER_STRIDE) == 0 and tm % MLSTM_CHUNK == 0
    assert d % (nh * LANES) == 0 and 2 * d == 4 * PROJ_CHUNK
    row = lambda a: a.reshape(1, -1).astype(F32)

    xf = x.reshape(t, d)
    for l in range(depth):
        wt = _scaled_rows(w_in[l], norm_mix_g[l]).T
        wqk = wt[:2 * d]
        wvo_t = wt[2 * d:4 * d]
        wb = wt[4 * d + n_gate:]
        wif_t = wt[4 * d:4 * d + n_gate]

        qk, vs, uc, gates, gz = _inproj(
            xf, wqk, wvo_t, wb, wif_t, conv_qk_w[l].astype(F32), row(conv_qk_b[l]),
            conf_conv_w[l].astype(F32), row(conf_conv_b[l]), row(conf_ln_g[l]),
            row(conf_ln_b[l]), seq=seq, tm=tm)

        ab, cm = _gates(gz, b_if[l].reshape(n_gate, 1).astype(F32))
        hm = _mlstm(qk, vs, ab, cm, ab.T, batch=batch, seq=seq, ts=2 * tm)

        xf = _tail(xf, hm, uc, gates, p[l].reshape(t, -1),
                   _scaled_rows(w_branch_m[l], mh_norm_g[l]), w_branch_c[l].astype(BF16),
                   w_out[l].astype(BF16), _scaled_rows(w_ffn_gate[l], norm_ffn_g[l]),
                   _scaled_rows(w_ffn_up[l], norm_ffn_g[l]), w_ffn_down[l].astype(BF16),
                   _scaled_rows(w_ple_gate[l], norm_ple_g[l]), w_ple_proj[l].astype(BF16),
                   row(final_g), tm=tm, final_norm=(l == depth - 1))
    return xf.reshape(batch, seq, d)
```

```python
import functools

import jax
import jax.numpy as jnp
from jax import lax
from jax.experimental import pallas as pl
from jax.experimental.pallas import tpu as pltpu

EPS = 1e-6
M_HEADS = 4
MLSTM_CHUNK = 256
SUBLANES = 8
LANES = 128
MXU_TILE = 256
GATHER_STRIDE = 4
GATHER_ROWS = 8
QK_HALO = 8
CONF_HALO = 32
PROJ_CHUNK = 512
VMEM_LIMIT = 60 * 1024 * 1024

BF16 = jnp.bfloat16
F32 = jnp.float32
NT_DIMS = (((1,), (1,)), ((), ()))


def _sigmoid(z):
    return 0.5 * jnp.tanh(0.5 * z) + 0.5


def _silu(z):
    h = 0.5 * z
    return h * jnp.tanh(h) + h


def _sigmoid_bf16(z):
    return _sigmoid(z.astype(BF16))


def _silu_bf16(z):
    return _silu(z.astype(BF16))


def _rms(xf):
    return xf * lax.rsqrt(jnp.mean(xf * xf, axis=-1, keepdims=True) + EPS)


def _scaled_rows(w, g):
    return (w.astype(F32) * g.astype(F32)[:, None]).astype(BF16)


def _resident(shape):
    nd = len(shape)
    return pl.BlockSpec(shape, lambda *_: (0,) * nd, pipeline_mode=pl.Buffered(1))


def _params(*semantics):
    return pltpu.CompilerParams(dimension_semantics=semantics, vmem_limit_bytes=VMEM_LIMIT)


def _fill_slabs(buf, val, first, *, slab0, halo, rows):
    for j in range(val.shape[1] // LANES):
        s = slab0 + j
        prev = buf[s, pl.ds(rows, halo), :]
        buf[s, pl.ds(0, halo), :] = jnp.where(first, 0.0, prev)
        buf[s, pl.ds(halo, rows), :] = val[:, j * LANES:(j + 1) * LANES]


def _causal_conv(src, dst, w_ref, b_ref, *, slabs, rows, halo):
    taps = w_ref.shape[0]
    st = GATHER_STRIDE
    span = GATHER_ROWS * st
    for s in slabs:
        cs = slice(s * LANES, (s + 1) * LANES)
        w = [jnp.broadcast_to(w_ref[k:k + 1, cs], (GATHER_ROWS, LANES)) for k in range(taps)]
        bias = jnp.broadcast_to(b_ref[0:1, cs], (GATHER_ROWS, LANES))
        for g in range(rows // span):
            acc = [bias] * st
            for m in range(st + taps - 1):
                uv = src[s, pl.ds(halo + g * span + m - (taps - 1), GATHER_ROWS, stride=st), :]
                for a in range(st):
                    if 0 <= m - a < taps:
                        acc[a] = acc[a] + w[m - a] * uv
            for a in range(st):
                dst[s, pl.ds(g * span + a, GATHER_ROWS, stride=st), :] = acc[a]


def _from_slabs(buf, lo, hi):
    return jnp.concatenate([buf[s] for s in range(lo, hi)], axis=-1)


def _inproj_kernel(*refs, tm, d, tiles_per_seq, q_scale, cast_scaled):
    n_cast = len(cast_scaled)
    (x_ref, wqk_ref, wvo_ref, wb_ref, wif_ref, qw_ref, qb_ref, cw_ref, cb_ref, lg_ref,
     lb_ref) = refs[:11]
    cast_in = list(refs[11:11 + n_cast + sum(cast_scaled)])
    outs = refs[11 + len(cast_in):]
    qk_ref, vs_ref, uc_ref, gate_ref, if_ref = outs[:5]
    cast_out = outs[5:5 + n_cast]
    qbuf, ubuf, ybuf_q, ybuf_c = outs[5 + n_cast:]
    i = pl.program_id(0)

    for scaled, o_ref in zip(cast_scaled, cast_out):
        w = cast_in.pop(0)[...]
        o_ref[...] = (w * cast_in.pop(0)[...] if scaled else w).astype(BF16)

    nsl = d // LANES
    pc = PROJ_CHUNK
    spc = pc // LANES

    @pl.when(i == 0)
    def _():
        qbuf[:, pl.ds(tm, QK_HALO), :] = jnp.zeros((2 * nsl, QK_HALO, LANES), F32)
        ubuf[:, pl.ds(tm, CONF_HALO), :] = jnp.zeros((nsl, CONF_HALO, LANES), F32)

    first = i % tiles_per_seq == 0
    h = _rms(x_ref[...]).astype(BF16)

    def proj(wt_ref, col0, n):
        return lax.dot_general(h, wt_ref[col0:col0 + n, :], NT_DIMS, preferred_element_type=F32)

    def glu(c):
        hw = pc // 2
        a = proj(wb_ref, c * hw, hw)
        ga = proj(wb_ref, d + c * hw, hw)
        _fill_slabs(ubuf, a * _sigmoid(ga), first, slab0=c * (spc // 2), halo=CONF_HALO, rows=tm)

    def conf_conv(c):
        _causal_conv(ubuf, ybuf_c, cw_ref, cb_ref, rows=tm, halo=CONF_HALO,
                     slabs=range(c * (spc // 2), (c + 1) * (spc // 2)))

    def qk(c):
        _fill_slabs(qbuf, proj(wqk_ref, c * pc, pc), first, slab0=c * spc, halo=QK_HALO, rows=tm)

    def qk_conv(c):
        _causal_conv(qbuf, ybuf_q, qw_ref, qb_ref, rows=tm, halo=QK_HALO,
                     slabs=range(c * spc, (c + 1) * spc))

    def vo(c):
        r = lax.dot_general(wvo_ref[c * pc:(c + 1) * pc, :], h, NT_DIMS, preferred_element_type=F32)
        vs_ref[c * pc:(c + 1) * pc, :] = r.astype(BF16) if (c + 1) * pc <= d else _sigmoid_bf16(r)

    def gate(c):
        gate_ref[:, c * pc:(c + 1) * pc] = _sigmoid_bf16(proj(wb_ref, 2 * d + c * pc, pc))

    def conf_norm():
        y = _from_slabs(ybuf_c, 0, nsl)
        yc = y - jnp.mean(y, axis=-1, keepdims=True)
        var = jnp.mean(yc * yc, axis=-1, keepdims=True)
        uc_ref[...] = _silu_bf16(yc * lax.rsqrt(var + EPS) * lg_ref[...] + lb_ref[...])

    assert 2 * d // pc == 4
    glu(0)
    glu(1); conf_conv(0)
    glu(2); qk(0); conf_conv(1)
    glu(3); qk(1); conf_conv(2)
    qk(2); qk(3); conf_conv(3)
    vo(0); vo(1); qk_conv(0)
    conf_norm(); vo(2); qk_conv(1)
    vo(3); gate(0); qk_conv(2)
    gate(1); qk_conv(3)
    qk_ref[:, :d] = _silu_bf16(_from_slabs(ybuf_q, 0, nsl)) * q_scale
    gate(2)
    qk_ref[:, d:] = _silu_bf16(_from_slabs(ybuf_q, nsl, 2 * nsl))
    gate(3)
    if_ref[...] = lax.dot_general(wif_ref[...], h, NT_DIMS, preferred_element_type=F32)


def _inproj(x2, wqk, wvo_t, wb, wif_t, qw, qb, cw, cb, lg, lb, *, seq, tm, cast):
    t, d = x2.shape
    nsl = d // LANES
    steps = t // tm
    cast_scaled = tuple(g is not None for _, g in cast)
    kern = functools.partial(_inproj_kernel, tm=tm, d=d, tiles_per_seq=seq // tm,
                             q_scale=float((d // M_HEADS) ** -0.5), cast_scaled=cast_scaled)
    tok = lambda width: pl.BlockSpec((tm, width), lambda i: (i, 0))
    feat = lambda rows: pl.BlockSpec((rows, tm), lambda i: (0, i))
    n_gate = wif_t.shape[0]
    cast_args, cast_specs, cast_out_specs, cast_out_shape = [], [], [], []
    for w, g in cast:
        rows, n = w.shape
        assert rows % (steps * 2 * SUBLANES) == 0
        blk = pl.BlockSpec((rows // steps, n), lambda i: (i, 0))
        cast_args.append(w)
        cast_specs.append(blk)
        if g is not None:
            cast_args.append(g.astype(F32).reshape(rows, 1))
            cast_specs.append(pl.BlockSpec((rows // steps, 1), lambda i: (i, 0)))
        cast_out_specs.append(blk)
        cast_out_shape.append(jax.ShapeDtypeStruct((rows, n), BF16))
    out_shape = [jax.ShapeDtypeStruct((t, 2 * d), BF16),
                 jax.ShapeDtypeStruct((2 * d, t), BF16),
                 jax.ShapeDtypeStruct((t, d), BF16),
                 jax.ShapeDtypeStruct((t, 2 * d), BF16),
                 jax.ShapeDtypeStruct((n_gate, t), F32)]
    consts = (wqk, wvo_t, wb, wif_t, qw, qb, cw, cb, lg, lb)
    res = pl.pallas_call(
        kern,
        grid=(steps,),
        in_specs=[tok(d)] + [_resident(a.shape) for a in consts] + cast_specs,
        out_specs=[tok(2 * d), feat(2 * d), tok(d), tok(2 * d), feat(n_gate)] + cast_out_specs,
        out_shape=out_shape + cast_out_shape,
        scratch_shapes=[pltpu.VMEM((2 * nsl, tm + QK_HALO, LANES), F32),
                        pltpu.VMEM((nsl, tm + CONF_HALO, LANES), F32),
                        pltpu.VMEM((2 * nsl, tm, LANES), F32),
                        pltpu.VMEM((nsl, tm, LANES), F32)],
        compiler_params=_params("arbitrary"),
        name="inproj",
    )(x2, *consts, *cast_args)
    return res[:5], res[5:]


def _chunk_scan(x, op, identity):
    lane_in_chunk = lax.broadcasted_iota(jnp.int32, x.shape, 1) % MLSTM_CHUNK
    sh = 1
    while sh < MLSTM_CHUNK:
        x = op(x, jnp.where(lane_in_chunk >= sh, pltpu.roll(x, sh, axis=1), identity))
        sh *= 2
    return x


def _gates_kernel(z_ref, bif_ref, ab_ref, cm_ref):
    nh = M_HEADS
    pre = z_ref[...] + bif_ref[...]
    lf = jnp.minimum(pre, 0.0) - jnp.log(1.0 + jnp.exp(-jnp.abs(pre)))
    b = _chunk_scan(lf, jnp.add, 0.0)
    a = pre - pltpu.roll(b, nh, axis=0)
    is_i = lax.broadcasted_iota(jnp.int32, pre.shape, 0) < nh
    ab_ref[...] = jnp.where(is_i, a, b)
    cm_ref[...] = _chunk_scan(a, jnp.maximum, -jnp.inf)


def _gates(gz, bif):
    full = lambda a: pl.BlockSpec(a.shape, lambda i: (0,) * a.ndim)
    out = jax.ShapeDtypeStruct(gz.shape, F32)
    return pl.pallas_call(
        _gates_kernel,
        grid=(1,),
        in_specs=[full(gz), full(bif)],
        out_specs=[full(gz), full(gz)],
        out_shape=[out, out],
        name="gates",
    )(gz, bif)


def _mlstm_kernel(qk_ref, vs_ref, ab_ref, cm_ref, acol_ref, o_ref, ct_ref, m_ref, *, ts, dh):
    L = MLSTM_CHUNK
    nh = M_HEADS
    dm = nh * dh
    j = pl.program_id(1)

    @pl.when(j == 0)
    def _():
        ct_ref[...] = jnp.zeros(ct_ref.shape, F32)
        m_ref[...] = jnp.zeros(m_ref.shape, F32)

    src = lax.broadcasted_iota(jnp.int32, (L, L), 0)
    tgt = lax.broadcasted_iota(jnp.int32, (L, L), 1)
    causal = src <= tgt
    ones_row = (lax.broadcasted_iota(jnp.int32, (LANES, L), 0) == 0).astype(BF16)

    m_state = [m_ref[hh:hh + 1, 0:1] for hh in range(nh)]
    for ci in range(ts // L):
        ts_ = slice(ci * L, (ci + 1) * L)
        for hh in range(nh):
            fs = slice(hh * dh, (hh + 1) * dh)
            a_row = ab_ref[hh:hh + 1, ts_]
            b_row = ab_ref[nh + hh:nh + hh + 1, ts_]
            m_prev = m_state[hh]
            c_row = jnp.maximum(cm_ref[hh:hh + 1, ts_], m_prev)
            a_col = acol_ref[ts_, hh:hh + 1]
            p_t = jnp.exp(jnp.where(causal, a_col - c_row, -jnp.inf))
            w_inter = jnp.exp(m_prev - c_row)

            qh = qk_ref[ts_, fs]
            kh = qk_ref[ts_, dm + hh * dh:dm + (hh + 1) * dh]
            vt_aug = jnp.concatenate([vs_ref[fs, ts_], ones_row], axis=0)
            ct = ct_ref[hh]
            both = lax.dot_general(jnp.concatenate([kh, ct.astype(BF16)], axis=0), qh, NT_DIMS,
                                   preferred_element_type=F32)
            sg_t = (both[:L] * p_t).astype(BF16)
            r_t = jnp.dot(vt_aug, sg_t, preferred_element_type=F32) + w_inter * both[L:]
            den = r_t[dh:dh + 1, :]
            m_t = b_row + c_row
            hv = r_t[:dh] * (1.0 / jnp.maximum(jnp.abs(den), jnp.exp(-m_t)))
            scale = lax.rsqrt(jnp.mean(hv * hv, axis=0, keepdims=True) + EPS)
            out_t = hv * scale * vs_ref[dm + hh * dh:dm + (hh + 1) * dh, ts_].astype(F32)
            o_ref[ts_, fs] = out_t.T.astype(BF16)

            c_last = c_row[:, L - 1:L]
            wk_row = jnp.exp(a_row - c_last)
            decay = jnp.exp(m_prev - c_last)
            vw = vt_aug * wk_row.astype(BF16)
            ct_ref[hh] = decay * ct + jnp.dot(vw, kh, preferred_element_type=F32)
            m_state[hh] = b_row[:, L - 1:L] + c_last
    for hh in range(nh):
        m_ref[hh:hh + 1, :] = jnp.broadcast_to(m_state[hh], (1, LANES))


def _mlstm(qk, vs, ab, cm, acol, *, batch, seq, ts):
    t = qk.shape[0]
    dm = qk.shape[1] // 2
    dh = dm // M_HEADS
    ns = seq // ts
    tok = lambda width: pl.BlockSpec((ts, width), lambda b, j: (b * ns + j, 0))
    feat = lambda rows: pl.BlockSpec((rows, ts), lambda b, j: (0, b * ns + j))
    kern = functools.partial(_mlstm_kernel, ts=ts, dh=dh)
    return pl.pallas_call(
        kern,
        grid=(batch, ns),
        in_specs=[tok(2 * dm), feat(2 * dm), feat(ab.shape[0]), feat(cm.shape[0]),
                  tok(acol.shape[1])],
        out_specs=tok(dm),
        out_shape=jax.ShapeDtypeStruct((t, dm), BF16),
        scratch_shapes=[pltpu.VMEM((M_HEADS, dh + LANES, dh), F32),
                        pltpu.VMEM((SUBLANES, LANES), F32)],
        compiler_params=_params("arbitrary", "arbitrary"),
        name="mlstm",
    )(qk, vs, ab, cm, acol)


def _tail_kernel(x_ref, hm_ref, uc_ref, gate_ref, p_ref, wm_ref, wc_ref, wo_ref,
                 wg_ref, wu_ref, wd_ref, wpg_ref, wpp_ref, fg_ref, o_ref,
                 *, d, ff_splits, final_norm):
    bm = jnp.dot(hm_ref[...], wm_ref[...], preferred_element_type=F32)
    bc = jnp.dot(uc_ref[...], wc_ref[...], preferred_element_type=F32)
    mix = gate_ref[:, :d] * bm.astype(BF16) + gate_ref[:, d:] * bc.astype(BF16)
    x1 = x_ref[...] + jnp.dot(mix, wo_ref[...], preferred_element_type=F32)

    f = _rms(x1).astype(BF16)
    acc = x1
    for lo, hi in zip(ff_splits[:-1], ff_splits[1:]):
        gt = jnp.dot(f, wg_ref[:, lo:hi], preferred_element_type=F32)
        up = jnp.dot(f, wu_ref[:, lo:hi], preferred_element_type=F32)
        hid = _silu_bf16(gt) * up.astype(BF16)
        acc = acc + jnp.dot(hid, wd_ref[lo:hi, :], preferred_element_type=F32)
    x2 = acc
    gate = _sigmoid(jnp.dot(_rms(x2).astype(BF16), wpg_ref[...], preferred_element_type=F32))
    x3 = x2 + gate * jnp.dot(p_ref[...].astype(BF16), wpp_ref[...], preferred_element_type=F32)
    o_ref[...] = _rms(x3) * fg_ref[...] if final_norm else x3


def _tail(x2, hm, uc, gates, p2, wm, wc, wo, wg, wu, wd, wpg, wpp, fg, *, tm, final_norm):
    t, d = x2.shape
    dff = wg.shape[1]
    tiles = dff // MXU_TILE
    ff_splits = (0, -(-tiles // 2) * MXU_TILE, dff) if dff % MXU_TILE == 0 and tiles > 1 else (0, dff)
    kern = functools.partial(_tail_kernel, d=d, ff_splits=ff_splits, final_norm=final_norm)
    tok = lambda width: pl.BlockSpec((tm, width), lambda i: (i, 0))
    consts = (wm, wc, wo, wg, wu, wd, wpg, wpp, fg)
    return pl.pallas_call(
        kern,
        grid=(t // tm,),
        in_specs=[tok(d), tok(d), tok(d), tok(2 * d), tok(p2.shape[1])]
        + [_resident(a.shape) for a in consts],
        out_specs=tok(d),
        out_shape=jax.ShapeDtypeStruct((t, d), F32),
        compiler_params=_params("arbitrary"),
        name="tail",
    )(x2, hm, uc, gates, p2, *consts)


def kernel(x, p, norm_mix_g, w_in, b_if, conv_qk_w, conv_qk_b, mh_norm_g, conf_conv_w, conf_conv_b, conf_ln_g, conf_ln_b, w_branch_m, w_branch_c, w_out, norm_ffn_g, w_ffn_gate, w_ffn_up, w_ffn_down, norm_ple_g, w_ple_gate, w_ple_proj, final_g):
    batch, seq, d = x.shape
    depth = w_in.shape[0]
    t = batch * seq
    nh = M_HEADS
    n_gate = 2 * nh
    tm = 512
    assert seq % (2 * tm) == 0 and tm % (GATHER_ROWS * GATHER_STRIDE) == 0 and tm % MLSTM_CHUNK == 0
    assert d % (nh * LANES) == 0 and 2 * d == 4 * PROJ_CHUNK
    row = lambda a: a.reshape(1, -1).astype(F32)

    xf = x.reshape(t, d)
    for l in range(depth):
        wt = _scaled_rows(w_in[l], norm_mix_g[l]).T
        wqk = wt[:2 * d]
        wvo_t = wt[2 * d:4 * d]
        wb = wt[4 * d + n_gate:]
        wif_t = wt[4 * d:4 * d + n_gate]

        dff = w_ffn_down.shape[1]
        ple = w_ple_proj.shape[1]
        tail_w = [(w_branch_m[l], mh_norm_g[l]), (w_branch_c[l], None), (w_out[l], None),
                  (w_ffn_gate[l], norm_ffn_g[l]), (w_ffn_up[l], norm_ffn_g[l]),
                  (w_ffn_down[l].reshape(d, dff), None), (w_ple_gate[l], norm_ple_g[l]),
                  (w_ple_proj[l].reshape(d, ple), None)]

        (qk, vs, uc, gates, gz), (wm, wc, wo, wg, wu, wd, wpg, wpp) = _inproj(
            xf, wqk, wvo_t, wb, wif_t, conv_qk_w[l].astype(F32), row(conv_qk_b[l]),
            conf_conv_w[l].astype(F32), row(conf_conv_b[l]), row(conf_ln_g[l]),
            row(conf_ln_b[l]), seq=seq, tm=tm, cast=tail_w)

        ab, cm = _gates(gz, b_if[l].reshape(n_gate, 1).astype(F32))
        hm = _mlstm(qk, vs, ab, cm, ab.T, batch=batch, seq=seq, ts=2 * tm)

        xf = _tail(xf, hm, uc, gates, p[l].reshape(t, -1), wm, wc, wo, wg, wu,
                   wd.reshape(dff, d), wpg, wpp.reshape(ple, d), row(final_g),
                   tm=tm, final_norm=(l == depth - 1))
    return xf.reshape(batch, seq, d)
```

```python
import functools

import jax
import jax.numpy as jnp
from jax import lax
from jax.experimental import pallas as pl
from jax.experimental.pallas import tpu as pltpu

EPS = 1e-6
M_HEADS = 4
MLSTM_CHUNK = 256
SUBLANES = 8
LANES = 128
MXU_TILE = 256
GATHER_STRIDE = 4
GATHER_ROWS = 8
QK_HALO = 8
CONF_HALO = 32
PROJ_CHUNK = 512
VMEM_LIMIT = 60 * 1024 * 1024

BF16 = jnp.bfloat16
F32 = jnp.float32
NT_DIMS = (((1,), (1,)), ((), ()))


def _sigmoid(z):
    return 0.5 * jnp.tanh(0.5 * z) + 0.5


def _silu(z):
    h = 0.5 * z
    return h * jnp.tanh(h) + h


def _sigmoid_bf16(z):
    return _sigmoid(z.astype(BF16))


def _silu_bf16(z):
    return _silu(z.astype(BF16))


def _rms(xf):
    return xf * lax.rsqrt(jnp.mean(xf * xf, axis=-1, keepdims=True) + EPS)


def _scaled_rows(w, g):
    return (w.astype(F32) * g.astype(F32)[:, None]).astype(BF16)


def _resident(shape):
    nd = len(shape)
    return pl.BlockSpec(shape, lambda *_: (0,) * nd, pipeline_mode=pl.Buffered(1))


def _params(*semantics):
    return pltpu.CompilerParams(dimension_semantics=semantics, vmem_limit_bytes=VMEM_LIMIT)


def _fill_slabs(buf, val, first, *, slab0, halo, rows):
    for j in range(val.shape[1] // LANES):
        s = slab0 + j
        prev = buf[s, pl.ds(rows, halo), :]
        buf[s, pl.ds(0, halo), :] = jnp.where(first, 0.0, prev)
        buf[s, pl.ds(halo, rows), :] = val[:, j * LANES:(j + 1) * LANES]


def _causal_conv(src, dst, w_ref, b_ref, *, slabs, rows, halo):
    taps = w_ref.shape[0]
    st = GATHER_STRIDE
    span = GATHER_ROWS * st
    for s in slabs:
        cs = slice(s * LANES, (s + 1) * LANES)
        w = [jnp.broadcast_to(w_ref[k:k + 1, cs], (GATHER_ROWS, LANES)) for k in range(taps)]
        bias = jnp.broadcast_to(b_ref[0:1, cs], (GATHER_ROWS, LANES))
        for g in range(rows // span):
            acc = [bias] * st
            for m in range(st + taps - 1):
                uv = src[s, pl.ds(halo + g * span + m - (taps - 1), GATHER_ROWS, stride=st), :]
                for a in range(st):
                    if 0 <= m - a < taps:
                        acc[a] = acc[a] + w[m - a] * uv
            for a in range(st):
                dst[s, pl.ds(g * span + a, GATHER_ROWS, stride=st), :] = acc[a]


def _from_slabs(buf, lo, hi):
    return jnp.concatenate([buf[s] for s in range(lo, hi)], axis=-1)


def _inproj_kernel(*refs, tm, d, tiles_per_seq, q_scale, cast_scaled):
    n_cast = len(cast_scaled)
    (x_ref, wqk_ref, wvo_ref, wb_ref, wif_ref, qw_ref, qb_ref, cw_ref, cb_ref, lg_ref,
     lb_ref) = refs[:11]
    cast_in = list(refs[11:11 + n_cast + sum(cast_scaled)])
    outs = refs[11 + len(cast_in):]
    qk_ref, vs_ref, uc_ref, gate_ref, if_ref = outs[:5]
    cast_out = outs[5:5 + n_cast]
    qbuf, ubuf, ybuf_q, ybuf_c = outs[5 + n_cast:]
    i = pl.program_id(0)

    for scaled, o_ref in zip(cast_scaled, cast_out):
        w = cast_in.pop(0)[...]
        o_ref[...] = (w * cast_in.pop(0)[...] if scaled else w).astype(BF16)

    nsl = d // LANES
    pc = PROJ_CHUNK
    spc = pc // LANES

    @pl.when(i == 0)
    def _():
        qbuf[:, pl.ds(tm, QK_HALO), :] = jnp.zeros((2 * nsl, QK_HALO, LANES), F32)
        ubuf[:, pl.ds(tm, CONF_HALO), :] = jnp.zeros((nsl, CONF_HALO, LANES), F32)

    first = i % tiles_per_seq == 0
    h = _rms(x_ref[...]).astype(BF16)

    def proj(wt_ref, col0, n):
        return lax.dot_general(h, wt_ref[col0:col0 + n, :], NT_DIMS, preferred_element_type=F32)

    def glu(c):
        hw = pc // 2
        a = proj(wb_ref, c * hw, hw)
        ga = proj(wb_ref, d + c * hw, hw)
        _fill_slabs(ubuf, a * _sigmoid(ga), first, slab0=c * (spc // 2), halo=CONF_HALO, rows=tm)

    def conf_conv(c):
        _causal_conv(ubuf, ybuf_c, cw_ref, cb_ref, rows=tm, halo=CONF_HALO,
                     slabs=range(c * (spc // 2), (c + 1) * (spc // 2)))

    def qk(c):
        _fill_slabs(qbuf, proj(wqk_ref, c * pc, pc), first, slab0=c * spc, halo=QK_HALO, rows=tm)

    def qk_conv(c):
        _causal_conv(qbuf, ybuf_q, qw_ref, qb_ref, rows=tm, halo=QK_HALO,
                     slabs=range(c * spc, (c + 1) * spc))

    def vo(c):
        r = lax.dot_general(wvo_ref[c * pc:(c + 1) * pc, :], h, NT_DIMS, preferred_element_type=F32)
        vs_ref[c * pc:(c + 1) * pc, :] = r.astype(BF16) if (c + 1) * pc <= d else _sigmoid_bf16(r)

    def gate(c):
        gate_ref[:, c * pc:(c + 1) * pc] = _sigmoid_bf16(proj(wb_ref, 2 * d + c * pc, pc))

    def conf_norm():
        y = _from_slabs(ybuf_c, 0, nsl)
        yc = y - jnp.mean(y, axis=-1, keepdims=True)
        var = jnp.mean(yc * yc, axis=-1, keepdims=True)
        uc_ref[...] = _silu_bf16(yc * lax.rsqrt(var + EPS) * lg_ref[...] + lb_ref[...])

    assert 2 * d // pc == 4
    glu(0)
    glu(1); conf_conv(0)
    glu(2); qk(0); conf_conv(1)
    glu(3); qk(1); conf_conv(2)
    qk(2); qk(3); conf_conv(3)
    vo(0); vo(1); qk_conv(0)
    conf_norm(); vo(2); qk_conv(1)
    vo(3); gate(0); qk_conv(2)
    gate(1); qk_conv(3)
    qk_ref[:, :d] = _silu_bf16(_from_slabs(ybuf_q, 0, nsl)) * q_scale
    gate(2)
    qk_ref[:, d:] = _silu_bf16(_from_slabs(ybuf_q, nsl, 2 * nsl))
    gate(3)
    if_ref[...] = lax.dot_general(wif_ref[...], h, NT_DIMS, preferred_element_type=F32)


def _inproj(x2, wqk, wvo_t, wb, wif_t, qw, qb, cw, cb, lg, lb, *, seq, tm, cast, layer):
    t, d = x2.shape
    nsl = d // LANES
    steps = t // tm
    cast_scaled = tuple(g is not None for _, g in cast)
    kern = functools.partial(_inproj_kernel, tm=tm, d=d, tiles_per_seq=seq // tm,
                             q_scale=float((d // M_HEADS) ** -0.5), cast_scaled=cast_scaled)
    tok = lambda width: pl.BlockSpec((tm, width), lambda i: (i, 0))
    feat = lambda rows: pl.BlockSpec((rows, tm), lambda i: (0, i))
    n_gate = wif_t.shape[0]
    cast_args, cast_specs, cast_out_specs, cast_out_shape = [], [], [], []
    for w, g in cast:
        _, rows, n = w.shape
        nblk = max(k for k in range(1, steps + 1)
                   if rows % k == 0 and (rows // k) % (2 * SUBLANES) == 0)
        at = lambda i, nblk=nblk: jnp.minimum(i, nblk - 1)
        cast_args.append(w)
        cast_specs.append(pl.BlockSpec((None, rows // nblk, n), lambda i, at=at: (layer, at(i), 0)))
        if g is not None:
            cast_args.append(g.astype(F32).reshape(rows, 1))
            cast_specs.append(pl.BlockSpec((rows // nblk, 1), lambda i, at=at: (at(i), 0)))
        cast_out_specs.append(pl.BlockSpec((rows // nblk, n), lambda i, at=at: (at(i), 0)))
        cast_out_shape.append(jax.ShapeDtypeStruct((rows, n), BF16))
    out_shape = [jax.ShapeDtypeStruct((t, 2 * d), BF16),
                 jax.ShapeDtypeStruct((2 * d, t), BF16),
                 jax.ShapeDtypeStruct((t, d), BF16),
                 jax.ShapeDtypeStruct((t, 2 * d), BF16),
                 jax.ShapeDtypeStruct((n_gate, t), F32)]
    consts = (wqk, wvo_t, wb, wif_t, qw, qb, cw, cb, lg, lb)
    res = pl.pallas_call(
        kern,
        grid=(steps,),
        in_specs=[tok(d)] + [_resident(a.shape) for a in consts] + cast_specs,
        out_specs=[tok(2 * d), feat(2 * d), tok(d), tok(2 * d), feat(n_gate)] + cast_out_specs,
        out_shape=out_shape + cast_out_shape,
        scratch_shapes=[pltpu.VMEM((2 * nsl, tm + QK_HALO, LANES), F32),
                        pltpu.VMEM((nsl, tm + CONF_HALO, LANES), F32),
                        pltpu.VMEM((2 * nsl, tm, LANES), F32),
                        pltpu.VMEM((nsl, tm, LANES), F32)],
        compiler_params=_params("arbitrary"),
        name="inproj",
    )(x2, *consts, *cast_args)
    return res[:5], res[5:]


def _chunk_scan(x, op, identity):
    lane_in_chunk = lax.broadcasted_iota(jnp.int32, x.shape, 1) % MLSTM_CHUNK
    sh = 1
    while sh < MLSTM_CHUNK:
        x = op(x, jnp.where(lane_in_chunk >= sh, pltpu.roll(x, sh, axis=1), identity))
        sh *= 2
    return x


def _gates_kernel(z_ref, bif_ref, ab_ref, cm_ref):
    nh = M_HEADS
    pre = z_ref[...] + bif_ref[...]
    lf = jnp.minimum(pre, 0.0) - jnp.log(1.0 + jnp.exp(-jnp.abs(pre)))
    b = _chunk_scan(lf, jnp.add, 0.0)
    a = pre - pltpu.roll(b, nh, axis=0)
    is_i = lax.broadcasted_iota(jnp.int32, pre.shape, 0) < nh
    ab_ref[...] = jnp.where(is_i, a, b)
    cm_ref[...] = _chunk_scan(a, jnp.maximum, -jnp.inf)


def _gates(gz, bif):
    full = lambda a: pl.BlockSpec(a.shape, lambda i: (0,) * a.ndim)
    out = jax.ShapeDtypeStruct(gz.shape, F32)
    return pl.pallas_call(
        _gates_kernel,
        grid=(1,),
        in_specs=[full(gz), full(bif)],
        out_specs=[full(gz), full(gz)],
        out_shape=[out, out],
        name="gates",
    )(gz, bif)


def _mlstm_kernel(qk_ref, vs_ref, ab_ref, cm_ref, acol_ref, o_ref, ct_ref, m_ref, *, ts, dh):
    L = MLSTM_CHUNK
    nh = M_HEADS
    dm = nh * dh
    j = pl.program_id(1)

    @pl.when(j == 0)
    def _():
        ct_ref[...] = jnp.zeros(ct_ref.shape, F32)
        m_ref[...] = jnp.zeros(m_ref.shape, F32)

    src = lax.broadcasted_iota(jnp.int32, (L, L), 0)
    tgt = lax.broadcasted_iota(jnp.int32, (L, L), 1)
    causal = src <= tgt
    ones_row = (lax.broadcasted_iota(jnp.int32, (LANES, L), 0) == 0).astype(BF16)

    m_state = [m_ref[hh:hh + 1, 0:1] for hh in range(nh)]
    for ci in range(ts // L):
        ts_ = slice(ci * L, (ci + 1) * L)
        for hh in range(nh):
            fs = slice(hh * dh, (hh + 1) * dh)
            a_row = ab_ref[hh:hh + 1, ts_]
            b_row = ab_ref[nh + hh:nh + hh + 1, ts_]
            m_prev = m_state[hh]
            c_row = jnp.maximum(cm_ref[hh:hh + 1, ts_], m_prev)
            a_col = acol_ref[ts_, hh:hh + 1]
            p_t = jnp.exp(jnp.where(causal, a_col - c_row, -jnp.inf))
            w_inter = jnp.exp(m_prev - c_row)

            qh = qk_ref[ts_, fs]
            kh = qk_ref[ts_, dm + hh * dh:dm + (hh + 1) * dh]
            vt_aug = jnp.concatenate([vs_ref[fs, ts_], ones_row], axis=0)
            ct = ct_ref[hh]
            both = lax.dot_general(jnp.concatenate([kh, ct.astype(BF16)], axis=0), qh, NT_DIMS,
                                   preferred_element_type=F32)
            sg_t = (both[:L] * p_t).astype(BF16)
            r_t = jnp.dot(vt_aug, sg_t, preferred_element_type=F32) + w_inter * both[L:]
            den = r_t[dh:dh + 1, :]
            m_t = b_row + c_row
            hv = r_t[:dh] * (1.0 / jnp.maximum(jnp.abs(den), jnp.exp(-m_t)))
            scale = lax.rsqrt(jnp.mean(hv * hv, axis=0, keepdims=True) + EPS)
            out_t = hv * scale * vs_ref[dm + hh * dh:dm + (hh + 1) * dh, ts_].astype(F32)
            o_ref[ts_, fs] = out_t.T.astype(BF16)

            c_last = c_row[:, L - 1:L]
            wk_row = jnp.exp(a_row - c_last)
            decay = jnp.exp(m_prev - c_last)
            vw = vt_aug * wk_row.astype(BF16)
            ct_ref[hh] = decay * ct + jnp.dot(vw, kh, preferred_element_type=F32)
            m_state[hh] = b_row[:, L - 1:L] + c_last
    for hh in range(nh):
        m_ref[hh:hh + 1, :] = jnp.broadcast_to(m_state[hh], (1, LANES))


def _mlstm(qk, vs, ab, cm, acol, *, batch, seq, ts):
    t = qk.shape[0]
    dm = qk.shape[1] // 2
    dh = dm // M_HEADS
    ns = seq // ts
    tok = lambda width: pl.BlockSpec((ts, width), lambda b, j: (b * ns + j, 0))
    feat = lambda rows: pl.BlockSpec((rows, ts), lambda b, j: (0, b * ns + j))
    kern = functools.partial(_mlstm_kernel, ts=ts, dh=dh)
    return pl.pallas_call(
        kern,
        grid=(batch, ns),
        in_specs=[tok(2 * dm), feat(2 * dm), feat(ab.shape[0]), feat(cm.shape[0]),
                  tok(acol.shape[1])],
        out_specs=tok(dm),
        out_shape=jax.ShapeDtypeStruct((t, dm), BF16),
        scratch_shapes=[pltpu.VMEM((M_HEADS, dh + LANES, dh), F32),
                        pltpu.VMEM((SUBLANES, LANES), F32)],
        compiler_params=_params("arbitrary", "arbitrary"),
        name="mlstm",
    )(qk, vs, ab, cm, acol)


def _tail_kernel(x_ref, hm_ref, uc_ref, gate_ref, p_ref, wm_ref, wc_ref, wo_ref,
                 wg_ref, wu_ref, wd_ref, wpg_ref, wpp_ref, fg_ref, o_ref,
                 *, d, ff_splits, final_norm):
    bm = jnp.dot(hm_ref[...], wm_ref[...], preferred_element_type=F32)
    bc = jnp.dot(uc_ref[...], wc_ref[...], preferred_element_type=F32)
    mix = gate_ref[:, :d] * bm.astype(BF16) + gate_ref[:, d:] * bc.astype(BF16)
    x1 = x_ref[...] + jnp.dot(mix, wo_ref[...], preferred_element_type=F32)

    f = _rms(x1).astype(BF16)
    acc = x1
    for lo, hi in zip(ff_splits[:-1], ff_splits[1:]):
        gt = jnp.dot(f, wg_ref[:, lo:hi], preferred_element_type=F32)
        up = jnp.dot(f, wu_ref[:, lo:hi], preferred_element_type=F32)
        hid = _silu_bf16(gt) * up.astype(BF16)
        acc = acc + jnp.dot(hid, wd_ref[lo:hi, :], preferred_element_type=F32)
    x2 = acc
    gate = _sigmoid(jnp.dot(_rms(x2).astype(BF16), wpg_ref[...], preferred_element_type=F32))
    x3 = x2 + gate * jnp.dot(p_ref[...].astype(BF16), wpp_ref[...], preferred_element_type=F32)
    o_ref[...] = _rms(x3) * fg_ref[...] if final_norm else x3


def _tail(x2, hm, uc, gates, p2, wm, wc, wo, wg, wu, wd, wpg, wpp, fg, *, tm, final_norm):
    t, d = x2.shape
    dff = wg.shape[1]
    tiles = dff // MXU_TILE
    ff_splits = (0, -(-tiles // 2) * MXU_TILE, dff) if dff % MXU_TILE == 0 and tiles > 1 else (0, dff)
    kern = functools.partial(_tail_kernel, d=d, ff_splits=ff_splits, final_norm=final_norm)
    tok = lambda width: pl.BlockSpec((tm, width), lambda i: (i, 0))
    consts = (wm, wc, wo, wg, wu, wd, wpg, wpp, fg)
    return pl.pallas_call(
        kern,
        grid=(t // tm,),
        in_specs=[tok(d), tok(d), tok(d), tok(2 * d), tok(p2.shape[1])]
        + [_resident(a.shape) for a in consts],
        out_specs=tok(d),
        out_shape=jax.ShapeDtypeStruct((t, d), F32),
        compiler_params=_params("arbitrary"),
        name="tail",
    )(x2, hm, uc, gates, p2, *consts)


def kernel(x, p, norm_mix_g, w_in, b_if, conv_qk_w, conv_qk_b, mh_norm_g, conf_conv_w, conf_conv_b, conf_ln_g, conf_ln_b, w_branch_m, w_branch_c, w_out, norm_ffn_g, w_ffn_gate, w_ffn_up, w_ffn_down, norm_ple_g, w_ple_gate, w_ple_proj, final_g):
    batch, seq, d = x.shape
    depth = w_in.shape[0]
    t = batch * seq
    nh = M_HEADS
    n_gate = 2 * nh
    tm = 512
    assert seq % (2 * tm) == 0 and tm % (GATHER_ROWS * GATHER_STRIDE) == 0 and tm % MLSTM_CHUNK == 0
    assert d % (nh * LANES) == 0 and 2 * d == 4 * PROJ_CHUNK
    row = lambda a: a.reshape(1, -1).astype(F32)

    xf = x.reshape(t, d)
    for l in range(depth):
        wt = _scaled_rows(w_in[l], norm_mix_g[l]).T
        wqk = wt[:2 * d]
        wvo_t = wt[2 * d:4 * d]
        wb = wt[4 * d + n_gate:]
        wif_t = wt[4 * d:4 * d + n_gate]

        tail_w = [(w_branch_m, mh_norm_g[l]), (w_branch_c, None), (w_out, None),
                  (w_ffn_gate, norm_ffn_g[l]), (w_ffn_up, norm_ffn_g[l]), (w_ffn_down, None),
                  (w_ple_gate, norm_ple_g[l]), (w_ple_proj, None)]

        (qk, vs, uc, gates, gz), (wm, wc, wo, wg, wu, wd, wpg, wpp) = _inproj(
            xf, wqk, wvo_t, wb, wif_t, conv_qk_w[l].astype(F32), row(conv_qk_b[l]),
            conf_conv_w[l].astype(F32), row(conf_conv_b[l]), row(conf_ln_g[l]),
            row(conf_ln_b[l]), seq=seq, tm=tm, cast=tail_w, layer=l)

        ab, cm = _gates(gz, b_if[l].reshape(n_gate, 1).astype(F32))
        hm = _mlstm(qk, vs, ab, cm, ab.T, batch=batch, seq=seq, ts=2 * tm)

        xf = _tail(xf, hm, uc, gates, p[l].reshape(t, -1), wm, wc, wo, wg, wu, wd, wpg, wpp,
                   row(final_g), tm=tm, final_norm=(l == depth - 1))
    return xf.reshape(batch, seq, d)
```

```python
import functools

import jax
import jax.numpy as jnp
from jax import lax
from jax.experimental import pallas as pl
from jax.experimental.pallas import tpu as pltpu

EPS = 1e-6
M_HEADS = 4
MLSTM_CHUNK = 256
SUBLANES = 8
LANES = 128
MXU_TILE = 256
GATHER_STRIDE = 4
GATHER_ROWS = 8
QK_HALO = 8
CONF_HALO = 32
PROJ_CHUNK = 512
VMEM_LIMIT = 60 * 1024 * 1024

BF16 = jnp.bfloat16
F32 = jnp.float32
NT_DIMS = (((1,), (1,)), ((), ()))


def _sigmoid(z):
    return 0.5 * jnp.tanh(0.5 * z) + 0.5


def _silu(z):
    h = 0.5 * z
    return h * jnp.tanh(h) + h


def _sigmoid_bf16(z):
    return _sigmoid(z.astype(BF16))


def _silu_bf16(z):
    return _silu(z.astype(BF16))


def _rms(xf):
    return xf * lax.rsqrt(jnp.mean(xf * xf, axis=-1, keepdims=True) + EPS)


def _scaled_rows(w, g):
    return (w.astype(F32) * g.astype(F32)[:, None]).astype(BF16)


def _resident(shape):
    nd = len(shape)
    return pl.BlockSpec(shape, lambda *_: (0,) * nd, pipeline_mode=pl.Buffered(1))


def _params(*semantics):
    return pltpu.CompilerParams(dimension_semantics=semantics, vmem_limit_bytes=VMEM_LIMIT)


def _rider_specs(cast, steps, step_of, layer):
    args, in_specs, out_specs, out_shape = [], [], [], []
    for w, g in cast:
        _, rows, n = w.shape
        nblk = max(k for k in range(1, steps + 1)
                   if rows % k == 0 and (rows // k) % (2 * SUBLANES) == 0)
        at = lambda *idx, nblk=nblk: jnp.minimum(step_of(*idx), nblk - 1)
        args.append(w)
        in_specs.append(pl.BlockSpec((None, rows // nblk, n),
                                     lambda *idx, at=at: (layer, at(*idx), 0)))
        if g is not None:
            args.append(g.astype(F32).reshape(rows, 1))
            in_specs.append(pl.BlockSpec((rows // nblk, 1), lambda *idx, at=at: (at(*idx), 0)))
        out_specs.append(pl.BlockSpec((rows // nblk, n), lambda *idx, at=at: (at(*idx), 0)))
        out_shape.append(jax.ShapeDtypeStruct((rows, n), BF16))
    return args, in_specs, out_specs, out_shape


def _fill_slabs(buf, val, first, *, slab0, halo, rows):
    for j in range(val.shape[1] // LANES):
        s = slab0 + j
        prev = buf[s, pl.ds(rows, halo), :]
        buf[s, pl.ds(0, halo), :] = jnp.where(first, 0.0, prev)
        buf[s, pl.ds(halo, rows), :] = val[:, j * LANES:(j + 1) * LANES]


def _causal_conv(src, dst, w_ref, b_ref, *, slabs, rows, halo):
    taps = w_ref.shape[0]
    st = GATHER_STRIDE
    span = GATHER_ROWS * st
    for s in slabs:
        cs = slice(s * LANES, (s + 1) * LANES)
        w = [jnp.broadcast_to(w_ref[k:k + 1, cs], (GATHER_ROWS, LANES)) for k in range(taps)]
        bias = jnp.broadcast_to(b_ref[0:1, cs], (GATHER_ROWS, LANES))
        for g in range(rows // span):
            acc = [bias] * st
            for m in range(st + taps - 1):
                uv = src[s, pl.ds(halo + g * span + m - (taps - 1), GATHER_ROWS, stride=st), :]
                for a in range(st):
                    if 0 <= m - a < taps:
                        acc[a] = acc[a] + w[m - a] * uv
            for a in range(st):
                dst[s, pl.ds(g * span + a, GATHER_ROWS, stride=st), :] = acc[a]


def _from_slabs(buf, lo, hi):
    return jnp.concatenate([buf[s] for s in range(lo, hi)], axis=-1)


def _inproj_kernel(*refs, tm, d, tiles_per_seq, q_scale, cast_scaled):
    n_cast = len(cast_scaled)
    (x_ref, wqk_ref, wvo_ref, wb_ref, wif_ref, qw_ref, qb_ref, cw_ref, cb_ref, lg_ref,
     lb_ref) = refs[:11]
    cast_in = list(refs[11:11 + n_cast + sum(cast_scaled)])
    outs = refs[11 + len(cast_in):]
    qk_ref, vs_ref, uc_ref, gate_ref, if_ref = outs[:5]
    cast_out = outs[5:5 + n_cast]
    qbuf, ubuf, ybuf_q, ybuf_c = outs[5 + n_cast:]
    i = pl.program_id(0)

    for scaled, o_ref in zip(cast_scaled, cast_out):
        w = cast_in.pop(0)[...]
        o_ref[...] = (w * cast_in.pop(0)[...] if scaled else w).astype(BF16)

    nsl = d // LANES
    pc = PROJ_CHUNK
    spc = pc // LANES

    @pl.when(i == 0)
    def _():
        qbuf[:, pl.ds(tm, QK_HALO), :] = jnp.zeros((2 * nsl, QK_HALO, LANES), F32)
        ubuf[:, pl.ds(tm, CONF_HALO), :] = jnp.zeros((nsl, CONF_HALO, LANES), F32)

    first = i % tiles_per_seq == 0
    h = _rms(x_ref[...]).astype(BF16)

    def proj(wt_ref, col0, n):
        return lax.dot_general(h, wt_ref[col0:col0 + n, :], NT_DIMS, preferred_element_type=F32)

    def glu(c):
        hw = pc // 2
        a = proj(wb_ref, c * hw, hw)
        ga = proj(wb_ref, d + c * hw, hw)
        _fill_slabs(ubuf, a * _sigmoid(ga), first, slab0=c * (spc // 2), halo=CONF_HALO, rows=tm)

    def conf_conv(c):
        _causal_conv(ubuf, ybuf_c, cw_ref, cb_ref, rows=tm, halo=CONF_HALO,
                     slabs=range(c * (spc // 2), (c + 1) * (spc // 2)))

    def qk(c):
        _fill_slabs(qbuf, proj(wqk_ref, c * pc, pc), first, slab0=c * spc, halo=QK_HALO, rows=tm)

    def qk_conv(c):
        _causal_conv(qbuf, ybuf_q, qw_ref, qb_ref, rows=tm, halo=QK_HALO,
                     slabs=range(c * spc, (c + 1) * spc))

    def vo(c):
        r = lax.dot_general(wvo_ref[c * pc:(c + 1) * pc, :], h, NT_DIMS, preferred_element_type=F32)
        vs_ref[c * pc:(c + 1) * pc, :] = r.astype(BF16) if (c + 1) * pc <= d else _sigmoid_bf16(r)

    def gate(c):
        gate_ref[:, c * pc:(c + 1) * pc] = _sigmoid_bf16(proj(wb_ref, 2 * d + c * pc, pc))

    def conf_norm():
        y = _from_slabs(ybuf_c, 0, nsl)
        yc = y - jnp.mean(y, axis=-1, keepdims=True)
        var = jnp.mean(yc * yc, axis=-1, keepdims=True)
        uc_ref[...] = _silu_bf16(yc * lax.rsqrt(var + EPS) * lg_ref[...] + lb_ref[...])

    assert 2 * d // pc == 4
    glu(0)
    glu(1); conf_conv(0)
    glu(2); qk(0); conf_conv(1)
    glu(3); qk(1); conf_conv(2)
    qk(2); qk(3); conf_conv(3)
    vo(0); vo(1); qk_conv(0)
    conf_norm(); vo(2); qk_conv(1)
    vo(3); gate(0); qk_conv(2)
    gate(1); qk_conv(3)
    qk_ref[:, :d] = _silu_bf16(_from_slabs(ybuf_q, 0, nsl)) * q_scale
    gate(2)
    qk_ref[:, d:] = _silu_bf16(_from_slabs(ybuf_q, nsl, 2 * nsl))
    gate(3)
    if_ref[...] = lax.dot_general(wif_ref[...], h, NT_DIMS, preferred_element_type=F32)


def _inproj(x2, wqk, wvo_t, wb, wif_t, qw, qb, cw, cb, lg, lb, *, seq, tm, cast, layer):
    t, d = x2.shape
    nsl = d // LANES
    steps = t // tm
    cast_scaled = tuple(g is not None for _, g in cast)
    kern = functools.partial(_inproj_kernel, tm=tm, d=d, tiles_per_seq=seq // tm,
                             q_scale=float((d // M_HEADS) ** -0.5), cast_scaled=cast_scaled)
    tok = lambda width: pl.BlockSpec((tm, width), lambda i: (i, 0))
    feat = lambda rows: pl.BlockSpec((rows, tm), lambda i: (0, i))
    n_gate = wif_t.shape[0]
    cast_args, cast_specs, cast_out_specs, cast_out_shape = _rider_specs(
        cast, steps, lambda i: i, layer)
    out_shape = [jax.ShapeDtypeStruct((t, 2 * d), BF16),
                 jax.ShapeDtypeStruct((2 * d, t), BF16),
                 jax.ShapeDtypeStruct((t, d), BF16),
                 jax.ShapeDtypeStruct((t, 2 * d), BF16),
                 jax.ShapeDtypeStruct((n_gate, t), F32)]
    consts = (wqk, wvo_t, wb, wif_t, qw, qb, cw, cb, lg, lb)
    res = pl.pallas_call(
        kern,
        grid=(steps,),
        in_specs=[tok(d)] + [_resident(a.shape) for a in consts] + cast_specs,
        out_specs=[tok(2 * d), feat(2 * d), tok(d), tok(2 * d), feat(n_gate)] + cast_out_specs,
        out_shape=out_shape + cast_out_shape,
        scratch_shapes=[pltpu.VMEM((2 * nsl, tm + QK_HALO, LANES), F32),
                        pltpu.VMEM((nsl, tm + CONF_HALO, LANES), F32),
                        pltpu.VMEM((2 * nsl, tm, LANES), F32),
                        pltpu.VMEM((nsl, tm, LANES), F32)],
        compiler_params=_params("arbitrary"),
        name="inproj",
    )(x2, *consts, *cast_args)
    return res[:5], res[5:]


def _chunk_scan(x, op, identity):
    lane_in_chunk = lax.broadcasted_iota(jnp.int32, x.shape, 1) % MLSTM_CHUNK
    sh = 1
    while sh < MLSTM_CHUNK:
        x = op(x, jnp.where(lane_in_chunk >= sh, pltpu.roll(x, sh, axis=1), identity))
        sh *= 2
    return x


def _gates_kernel(z_ref, bif_ref, ab_ref, cm_ref):
    nh = M_HEADS
    pre = z_ref[...] + bif_ref[...]
    lf = jnp.minimum(pre, 0.0) - jnp.log(1.0 + jnp.exp(-jnp.abs(pre)))
    b = _chunk_scan(lf, jnp.add, 0.0)
    a = pre - pltpu.roll(b, nh, axis=0)
    is_i = lax.broadcasted_iota(jnp.int32, pre.shape, 0) < nh
    ab_ref[...] = jnp.where(is_i, a, b)
    cm_ref[...] = _chunk_scan(a, jnp.maximum, -jnp.inf)


def _gates(gz, bif):
    full = lambda a: pl.BlockSpec(a.shape, lambda i: (0,) * a.ndim)
    out = jax.ShapeDtypeStruct(gz.shape, F32)
    return pl.pallas_call(
        _gates_kernel,
        grid=(1,),
        in_specs=[full(gz), full(bif)],
        out_specs=[full(gz), full(gz)],
        out_shape=[out, out],
        name="gates",
    )(gz, bif)


def _mlstm_kernel(*refs, ts, dh, cast_scaled):
    n_cast = len(cast_scaled)
    qk_ref, vs_ref, ab_ref, cm_ref, acol_ref = refs[:5]
    cast_in = list(refs[5:5 + n_cast + sum(cast_scaled)])
    outs = refs[5 + len(cast_in):]
    o_ref = outs[0]
    cast_out = outs[1:1 + n_cast]
    ct_ref, m_ref = outs[1 + n_cast:]

    for scaled, w_out_ref in zip(cast_scaled, cast_out):
        w = cast_in.pop(0)[...]
        w_out_ref[...] = (w * cast_in.pop(0)[...] if scaled else w).astype(BF16)

    L = MLSTM_CHUNK
    nh = M_HEADS
    dm = nh * dh
    j = pl.program_id(1)

    @pl.when(j == 0)
    def _():
        ct_ref[...] = jnp.zeros(ct_ref.shape, F32)
        m_ref[...] = jnp.zeros(m_ref.shape, F32)

    src = lax.broadcasted_iota(jnp.int32, (L, L), 0)
    tgt = lax.broadcasted_iota(jnp.int32, (L, L), 1)
    causal = src <= tgt
    ones_row = (lax.broadcasted_iota(jnp.int32, (LANES, L), 0) == 0).astype(BF16)

    m_state = [m_ref[hh:hh + 1, 0:1] for hh in range(nh)]
    for ci in range(ts // L):
        ts_ = slice(ci * L, (ci + 1) * L)
        for hh in range(nh):
            fs = slice(hh * dh, (hh + 1) * dh)
            a_row = ab_ref[hh:hh + 1, ts_]
            b_row = ab_ref[nh + hh:nh + hh + 1, ts_]
            m_prev = m_state[hh]
            c_row = jnp.maximum(cm_ref[hh:hh + 1, ts_], m_prev)
            a_col = acol_ref[ts_, hh:hh + 1]
            p_t = jnp.exp(jnp.where(causal, a_col - c_row, -jnp.inf))
            w_inter = jnp.exp(m_prev - c_row)

            qh = qk_ref[ts_, fs]
            kh = qk_ref[ts_, dm + hh * dh:dm + (hh + 1) * dh]
            vt_aug = jnp.concatenate([vs_ref[fs, ts_], ones_row], axis=0)
            ct = ct_ref[hh]
            both = lax.dot_general(jnp.concatenate([kh, ct.astype(BF16)], axis=0), qh, NT_DIMS,
                                   preferred_element_type=F32)
            sg_t = (both[:L] * p_t).astype(BF16)
            r_t = jnp.dot(vt_aug, sg_t, preferred_element_type=F32) + w_inter * both[L:]
            den = r_t[dh:dh + 1, :]
            m_t = b_row + c_row
            hv = r_t[:dh] * (1.0 / jnp.maximum(jnp.abs(den), jnp.exp(-m_t)))
            scale = lax.rsqrt(jnp.mean(hv * hv, axis=0, keepdims=True) + EPS)
            out_t = hv * scale * vs_ref[dm + hh * dh:dm + (hh + 1) * dh, ts_].astype(F32)
            o_ref[ts_, fs] = out_t.T.astype(BF16)

            c_last = c_row[:, L - 1:L]
            wk_row = jnp.exp(a_row - c_last)
            decay = jnp.exp(m_prev - c_last)
            vw = vt_aug * wk_row.astype(BF16)
            ct_ref[hh] = decay * ct + jnp.dot(vw, kh, preferred_element_type=F32)
            m_state[hh] = b_row[:, L - 1:L] + c_last
    for hh in range(nh):
        m_ref[hh:hh + 1, :] = jnp.broadcast_to(m_state[hh], (1, LANES))


def _mlstm(qk, vs, ab, cm, acol, *, batch, seq, ts, cast, layer):
    t = qk.shape[0]
    dm = qk.shape[1] // 2
    dh = dm // M_HEADS
    ns = seq // ts
    tok = lambda width: pl.BlockSpec((ts, width), lambda b, j: (b * ns + j, 0))
    feat = lambda rows: pl.BlockSpec((rows, ts), lambda b, j: (0, b * ns + j))
    cast_args, cast_specs, cast_out_specs, cast_out_shape = _rider_specs(
        cast, batch * ns, lambda b, j: b * ns + j, layer)
    kern = functools.partial(_mlstm_kernel, ts=ts, dh=dh,
                             cast_scaled=tuple(g is not None for _, g in cast))
    res = pl.pallas_call(
        kern,
        grid=(batch, ns),
        in_specs=[tok(2 * dm), feat(2 * dm), feat(ab.shape[0]), feat(cm.shape[0]),
                  tok(acol.shape[1])] + cast_specs,
        out_specs=[tok(dm)] + cast_out_specs,
        out_shape=[jax.ShapeDtypeStruct((t, dm), BF16)] + cast_out_shape,
        scratch_shapes=[pltpu.VMEM((M_HEADS, dh + LANES, dh), F32),
                        pltpu.VMEM((SUBLANES, LANES), F32)],
        compiler_params=_params("arbitrary", "arbitrary"),
        name="mlstm",
    )(qk, vs, ab, cm, acol, *cast_args)
    return res[0], res[1:]


def _tail_kernel(x_ref, hm_ref, uc_ref, gate_ref, p_ref, wm_ref, wc_ref, wo_ref,
                 wg_ref, wu_ref, wd_ref, wpg_ref, wpp_ref, fg_ref, o_ref,
                 *, d, ff_splits, final_norm):
    bm = jnp.dot(hm_ref[...], wm_ref[...], preferred_element_type=F32)
    bc = jnp.dot(uc_ref[...], wc_ref[...], preferred_element_type=F32)
    mix = gate_ref[:, :d] * bm.astype(BF16) + gate_ref[:, d:] * bc.astype(BF16)
    x1 = x_ref[...] + jnp.dot(mix, wo_ref[...], preferred_element_type=F32)

    f = _rms(x1).astype(BF16)
    acc = x1
    for lo, hi in zip(ff_splits[:-1], ff_splits[1:]):
        gt = jnp.dot(f, wg_ref[:, lo:hi], preferred_element_type=F32)
        up = jnp.dot(f, wu_ref[:, lo:hi], preferred_element_type=F32)
        hid = _silu_bf16(gt) * up.astype(BF16)
        acc = acc + jnp.dot(hid, wd_ref[lo:hi, :], preferred_element_type=F32)
    x2 = acc
    gate = _sigmoid(jnp.dot(_rms(x2).astype(BF16), wpg_ref[...], preferred_element_type=F32))
    x3 = x2 + gate * jnp.dot(p_ref[...].astype(BF16), wpp_ref[...], preferred_element_type=F32)
    o_ref[...] = _rms(x3) * fg_ref[...] if final_norm else x3


def _tail(x2, hm, uc, gates, p2, wm, wc, wo, wg, wu, wd, wpg, wpp, fg, *, tm, final_norm):
    t, d = x2.shape
    dff = wg.shape[1]
    tiles = dff // MXU_TILE
    ff_splits = (0, -(-tiles // 2) * MXU_TILE, dff) if dff % MXU_TILE == 0 and tiles > 1 else (0, dff)
    kern = functools.partial(_tail_kernel, d=d, ff_splits=ff_splits, final_norm=final_norm)
    tok = lambda width: pl.BlockSpec((tm, width), lambda i: (i, 0))
    consts = (wm, wc, wo, wg, wu, wd, wpg, wpp, fg)
    return pl.pallas_call(
        kern,
        grid=(t // tm,),
        in_specs=[tok(d), tok(d), tok(d), tok(2 * d), tok(p2.shape[1])]
        + [_resident(a.shape) for a in consts],
        out_specs=tok(d),
        out_shape=jax.ShapeDtypeStruct((t, d), F32),
        compiler_params=_params("arbitrary"),
        name="tail",
    )(x2, hm, uc, gates, p2, *consts)


def kernel(x, p, norm_mix_g, w_in, b_if, conv_qk_w, conv_qk_b, mh_norm_g, conf_conv_w, conf_conv_b, conf_ln_g, conf_ln_b, w_branch_m, w_branch_c, w_out, norm_ffn_g, w_ffn_gate, w_ffn_up, w_ffn_down, norm_ple_g, w_ple_gate, w_ple_proj, final_g):
    batch, seq, d = x.shape
    depth = w_in.shape[0]
    t = batch * seq
    nh = M_HEADS
    n_gate = 2 * nh
    tm = 512
    assert seq % (2 * tm) == 0 and tm % (GATHER_ROWS * GATHER_STRIDE) == 0 and tm % MLSTM_CHUNK == 0
    assert d % (nh * LANES) == 0 and 2 * d == 4 * PROJ_CHUNK
    row = lambda a: a.reshape(1, -1).astype(F32)

    xf = x.reshape(t, d)
    for l in range(depth):
        wt = _scaled_rows(w_in[l], norm_mix_g[l]).T
        wqk = wt[:2 * d]
        wvo_t = wt[2 * d:4 * d]
        wb = wt[4 * d + n_gate:]
        wif_t = wt[4 * d:4 * d + n_gate]

        tail_w = [(w_branch_m, mh_norm_g[l]), (w_branch_c, None), (w_out, None),
                  (w_ffn_gate, norm_ffn_g[l]), (w_ffn_up, norm_ffn_g[l]), (w_ffn_down, None),
                  (w_ple_gate, norm_ple_g[l]), (w_ple_proj, None)]

        (qk, vs, uc, gates, gz), _ = _inproj(
            xf, wqk, wvo_t, wb, wif_t, conv_qk_w[l].astype(F32), row(conv_qk_b[l]),
            conf_conv_w[l].astype(F32), row(conf_conv_b[l]), row(conf_ln_g[l]),
            row(conf_ln_b[l]), seq=seq, tm=tm, cast=[], layer=l)

        ab, cm = _gates(gz, b_if[l].reshape(n_gate, 1).astype(F32))
        hm, (wm, wc, wo, wg, wu, wd, wpg, wpp) = _mlstm(
            qk, vs, ab, cm, ab.T, batch=batch, seq=seq, ts=2 * tm, cast=tail_w, layer=l)

        xf = _tail(xf, hm, uc, gates, p[l].reshape(t, -1), wm, wc, wo, wg, wu, wd, wpg, wpp,
                   row(final_g), tm=tm, final_norm=(l == depth - 1))
    return xf.reshape(batch, seq, d)
```
